```python
import math
import jax
import jax.numpy as jnp
from jax import lax
import numpy as np

D_MODEL = 2048
BATCH = 2
SEQ = 16384
DEPTH = 1
DEC_BATCH = 8
DEC_SEQ = 64
PAST_LEN = 1024

CHUNK = 64
SSM_WIDTH = D_MODEL // 2
SSM_GROUP_CH = 16
SSM_GROUPS = SSM_WIDTH // SSM_GROUP_CH
SSM_STATE = 64
ATTN_WIDTH = D_MODEL - SSM_WIDTH
N_ATTN_HEADS = 8
ATTN_HEAD_DIM = ATTN_WIDTH // N_ATTN_HEADS // 2
ATTN_QK_DIM = 2 * ATTN_HEAD_DIM
ATTN_V_DIM = 2 * ATTN_HEAD_DIM
IN_WIDTH = SSM_WIDTH + 3 * ATTN_WIDTH
Q_BLOCK = 128
REL_BUCKETS = 32
REL_MAX_DIST = 128
N_EXPERT_GROUPS = 4
EXPERTS_PER_GROUP = 8
N_EXPERTS = N_EXPERT_GROUPS * EXPERTS_PER_GROUP
EXPERT_TOP_K = 2
D_EXPERT = D_MODEL // 4
MOE_BLOCK = 128
DN_ALPHA = (2 * DEPTH) ** 0.25
DN_BETA = (8 * DEPTH) ** -0.25
LN_EPS = 1e-5
RMS_EPS = 1e-5
NEG_INF = -1e30
F32 = jnp.float32

kernel_name = 'hymba_s5_diffattn_hmoe_stream'


def lambda_init(layer):
    return 0.8 - 0.6 * math.exp(-0.3 * layer)


def layer_norm(x, g, b):
    xf = x.astype(F32)
    xc = xf - jnp.mean(xf, axis=-1, keepdims=True)
    var = jnp.mean(xc * xc, axis=-1, keepdims=True)
    return (xc * lax.rsqrt(var + LN_EPS) * g.astype(F32) + b.astype(F32)).astype(x.dtype)


def rel_bucket(rel):
    nb = REL_BUCKETS // 2
    max_exact = nb // 2
    n = jnp.abs(rel)
    nf = jnp.maximum(n, 1).astype(F32)
    large = max_exact + (jnp.log(nf / max_exact) / math.log(REL_MAX_DIST / max_exact) * (nb - max_exact)).astype(jnp.int32)
    large = jnp.minimum(large, nb - 1)
    return jnp.where(rel > 0, nb, 0) + jnp.where(n < max_exact, n, large)


def diff_attention(q, k, v, qpos, kpos, rel_bias, lam):
    s = jnp.einsum('bqhmd,bkhmd->bmhqk', q.astype(F32), k.astype(F32)) * (ATTN_HEAD_DIM ** -0.5)
    bias = jnp.transpose(rel_bias.astype(F32)[rel_bucket(kpos[None, :] - qpos[:, None])], (2, 0, 1))
    mask = (kpos[None, :] // CHUNK) <= (qpos[:, None] // CHUNK)
    p = jax.nn.softmax(jnp.where(mask, s + bias, NEG_INF), axis=-1)
    a = p[:, 0] - lam * p[:, 1]
    return jnp.einsum('bhqk,bkhe->bqhe', a, v.astype(F32))


def diff_attn_prompt(q, k, v, rel_bias, lam):
    b, L = q.shape[:2]
    nq = L // Q_BLOCK
    qb = jnp.swapaxes(q.reshape(b, nq, Q_BLOCK, N_ATTN_HEADS, 2, ATTN_HEAD_DIM), 0, 1)
    pos = jnp.arange(L, dtype=jnp.int32)
    qpos = pos.reshape(nq, Q_BLOCK)
    o = lax.map(lambda a: diff_attention(a[0], k, v, a[1], pos, rel_bias, lam), (qb, qpos))
    return jnp.swapaxes(o, 0, 1).reshape(b, L, N_ATTN_HEADS, ATTN_V_DIM)


def diff_attn_post(o, g, lam_init):
    b, L = o.shape[:2]
    o = o * lax.rsqrt(jnp.mean(o * o, axis=-1, keepdims=True) + RMS_EPS) * g.astype(F32) * (1.0 - lam_init)
    return o.reshape(b, L, ATTN_WIDTH)


def ssm_discretize(a_re, a_im, log_dt, b_re, b_im):
    a_re = a_re.astype(F32)
    a_im = a_im.astype(F32)
    dt = jnp.exp(log_dt.astype(F32))[:, None]
    mag = jnp.exp(a_re * dt)
    ab_re = mag * jnp.cos(a_im * dt)
    ab_im = mag * jnp.sin(a_im * dt)
    nr = ab_re - 1.0
    ni = ab_im
    den = a_re * a_re + a_im * a_im
    c_re = ((nr * a_re + ni * a_im) / den)[..., None]
    c_im = ((ni * a_re - nr * a_im) / den)[..., None]
    b_re = b_re.astype(F32)
    b_im = b_im.astype(F32)
    bb_re = c_re * b_re - c_im * b_im
    bb_im = c_re * b_im + c_im * b_re
    return (ab_re, ab_im, bb_re, bb_im)


def _ssm_combine(e1, e2):
    a1r, a1i, b1r, b1i = e1
    a2r, a2i, b2r, b2i = e2
    return (a2r * a1r - a2i * a1i, a2r * a1i + a2i * a1r,
            a2r * b1r - a2i * b1i + b2r, a2r * b1i + a2i * b1r + b2i)


def ssm_chunk(h_re, h_im, u, ab_re, ab_im, bb_re, bb_im):
    bu_re = jnp.einsum('blgh,gph->blgp', u, bb_re)
    bu_im = jnp.einsum('blgh,gph->blgp', u, bb_im)
    a_re = jnp.broadcast_to(ab_re, bu_re.shape)
    a_im = jnp.broadcast_to(ab_im, bu_re.shape)
    A_re, A_im, S_re, S_im = lax.associative_scan(_ssm_combine, (a_re, a_im, bu_re, bu_im), axis=1)
    hr = A_re * h_re[:, None] - A_im * h_im[:, None] + S_re
    hi = A_re * h_im[:, None] + A_im * h_re[:, None] + S_im
    return hr, hi


def ssm_readout(hr, hi, u, c_re, c_im, d):
    return jnp.einsum('blgp,ghp->blgh', hr, c_re) - jnp.einsum('blgp,ghp->blgh', hi, c_im) + d * u


def ssm_prompt(u, disc, c_re, c_im, d):
    b, L = u.shape[:2]
    nc = L // CHUNK
    uc = jnp.swapaxes(u.reshape(b, nc, CHUNK, SSM_GROUPS, SSM_GROUP_CH), 0, 1)
    h0 = jnp.zeros((b, SSM_GROUPS, SSM_STATE), F32)

    def step(carry, u_c):
        hr, hi = ssm_chunk(carry[0], carry[1], u_c, *disc)
        return (hr[:, -1], hi[:, -1]), ssm_readout(hr, hi, u_c, c_re, c_im, d)

    (h_re, h_im), ys = lax.scan(step, (h0, h0), uc)
    return jnp.swapaxes(ys, 0, 1).reshape(b, L, SSM_GROUPS, SSM_GROUP_CH), h_re, h_im


def ssm_glu(y, w_glu, b_glu):
    b, L = y.shape[:2]
    g = jax.nn.gelu(y.reshape(b, L, SSM_WIDTH))
    return g * jax.nn.sigmoid(g @ w_glu.astype(F32) + b_glu.astype(F32))


def split_in(z):
    b, L = z.shape[:2]
    u = z[..., :SSM_WIDTH].reshape(b, L, SSM_GROUPS, SSM_GROUP_CH)
    q = z[..., SSM_WIDTH:SSM_WIDTH + ATTN_WIDTH].reshape(b, L, N_ATTN_HEADS, 2, ATTN_HEAD_DIM)
    k = z[..., SSM_WIDTH + ATTN_WIDTH:SSM_WIDTH + 2 * ATTN_WIDTH].reshape(b, L, N_ATTN_HEADS, 2, ATTN_HEAD_DIM)
    v = z[..., SSM_WIDTH + 2 * ATTN_WIDTH:].reshape(b, L, N_ATTN_HEADS, ATTN_V_DIM)
    return u, q, k, v


def hier_moe(x2d, w_r1, b_r1, w_r2, b_r2, w_gate, w_up, w_down):
    T = x2d.shape[0]
    K = EXPERT_TOP_K
    xf = x2d.astype(F32)
    p_grp = jax.nn.softmax(xf @ w_r1.astype(F32) + b_r1.astype(F32), axis=-1)
    w_grp, grp = lax.top_k(p_grp, 1)
    logits_in = jnp.einsum('td,dge->tge', xf, w_r2.astype(F32)) + b_r2.astype(F32)
    idx = jnp.broadcast_to(grp[:, :, None], (T, 1, EXPERTS_PER_GROUP))
    logits_in = jnp.take_along_axis(logits_in, idx, axis=1)[:, 0]
    top_val, top_idx = lax.top_k(logits_in, K)
    gate = w_grp * jax.nn.softmax(top_val, axis=-1)
    eid = (grp * EXPERTS_PER_GROUP + top_idx).reshape(T * K)
    A = T * K
    onehot = (eid[:, None] == jnp.arange(N_EXPERTS)[None, :]).astype(jnp.int32)
    rank = jnp.take_along_axis(jnp.cumsum(onehot, axis=0), eid[:, None], axis=1)[:, 0] - 1
    counts = jnp.sum(onehot, axis=0)
    padded = (counts + MOE_BLOCK - 1) // MOE_BLOCK * MOE_BLOCK
    pad_end = jnp.cumsum(padded)
    pad_start = pad_end - padded
    dest = pad_start[eid] + rank
    n_blocks = -(-A // MOE_BLOCK) + N_EXPERTS
    x_buf = jnp.zeros((n_blocks * MOE_BLOCK, D_MODEL), x2d.dtype).at[dest].set(jnp.repeat(x2d, K, axis=0))
    block_e = jnp.minimum(jnp.searchsorted(pad_end, jnp.arange(n_blocks) * MOE_BLOCK, side='right'), N_EXPERTS - 1)

    def run(args):
        xb, e = args
        return (jax.nn.silu(xb @ w_gate[e]) * (xb @ w_up[e])) @ w_down[e]

    y_buf = lax.map(run, (x_buf.reshape(n_blocks, MOE_BLOCK, D_MODEL), block_e)).reshape(n_blocks * MOE_BLOCK, D_MODEL)
    y = jnp.sum(y_buf[dest].reshape(T, K, D_MODEL).astype(F32) * gate[..., None], axis=1)
    return y.astype(x2d.dtype)


def layer_finish(x, ssm_out, attn_out, w_out, ln1_g, ln1_b, w_r1, b_r1, w_r2, b_r2, w_gate, w_up, w_down, ln2_g, ln2_b):
    mix = jnp.concatenate([ssm_out, attn_out], axis=-1).astype(x.dtype) @ w_out
    h = layer_norm(DN_ALPHA * x + mix, ln1_g, ln1_b)
    f = hier_moe(h.reshape(-1, D_MODEL), w_r1, b_r1, w_r2, b_r2, w_gate, w_up, w_down).reshape(h.shape)
    return layer_norm(DN_ALPHA * h + f, ln2_g, ln2_b)


def setup_inputs(seed: int = 0) -> dict:
    key = jax.random.key(seed)
    k = jax.random.split(key, 40)
    nrm = jax.random.normal
    H, G, P, C = N_ATTN_HEADS, SSM_GROUPS, SSM_STATE, SSM_GROUP_CH
    col_scale = jnp.where(jnp.arange(IN_WIDTH) >= SSM_WIDTH + 2 * ATTN_WIDTH, DN_BETA, 1.0).astype(F32)
    w_in = nrm(k[6], (DEPTH, D_MODEL, IN_WIDTH), F32) * (D_MODEL ** -0.5) * col_scale
    return {
        'x_prompt': nrm(k[0], (BATCH, SEQ, D_MODEL), F32),
        'x_sample': nrm(k[1], (DEC_BATCH, DEC_SEQ, D_MODEL), F32),
        'cache_k': nrm(k[2], (DEPTH, DEC_BATCH, PAST_LEN, H, ATTN_QK_DIM), F32),
        'cache_v': nrm(k[3], (DEPTH, DEC_BATCH, PAST_LEN, H, ATTN_V_DIM), F32) * DN_BETA,
        'state_ssm_re': 0.1 * nrm(k[4], (DEPTH, DEC_BATCH, G, P), F32),
        'state_ssm_im': 0.1 * nrm(k[5], (DEPTH, DEC_BATCH, G, P), F32),
        'w_in': w_in,
        'ssm_a_re': -0.5 + 0.01 * nrm(k[7], (DEPTH, G, P), F32),
        'ssm_a_im': jnp.pi * jnp.arange(P, dtype=F32) + 0.01 * nrm(k[8], (DEPTH, G, P), F32),
        'ssm_log_dt': jax.random.uniform(k[9], (DEPTH, G), F32, math.log(1e-3), math.log(1e-1)),
        'ssm_b_re': nrm(k[10], (DEPTH, G, P, C), F32) * ((2 * C) ** -0.5),
        'ssm_b_im': nrm(k[11], (DEPTH, G, P, C), F32) * ((2 * C) ** -0.5),
        'ssm_c_re': nrm(k[12], (DEPTH, G, C, P), F32) * (2.0 ** -0.5),
        'ssm_c_im': nrm(k[13], (DEPTH, G, C, P), F32) * (2.0 ** -0.5),
        'ssm_d': nrm(k[14], (DEPTH, G, C), F32),
        'w_glu': nrm(k[15], (DEPTH, SSM_WIDTH, SSM_WIDTH), F32) * (SSM_WIDTH ** -0.5),
        'b_glu': 0.01 * nrm(k[16], (DEPTH, SSM_WIDTH), F32),
        'lambda_q1': 0.1 * nrm(k[17], (DEPTH, ATTN_HEAD_DIM), F32),
        'lambda_k1': 0.1 * nrm(k[18], (DEPTH, ATTN_HEAD_DIM), F32),
        'lambda_q2': 0.1 * nrm(k[19], (DEPTH, ATTN_HEAD_DIM), F32),
        'lambda_k2': 0.1 * nrm(k[20], (DEPTH, ATTN_HEAD_DIM), F32),
        'subln_g': 1.0 + 0.01 * nrm(k[21], (DEPTH, ATTN_V_DIM), F32),
        'rel_bias': 0.1 * nrm(k[22], (REL_BUCKETS, H), F32),
        'w_out': nrm(k[23], (DEPTH, D_MODEL, D_MODEL), F32) * (D_MODEL ** -0.5) * DN_BETA,
        'ln1_g': 1.0 + 0.01 * nrm(k[24], (DEPTH, D_MODEL), F32),
        'ln1_b': 0.01 * nrm(k[25], (DEPTH, D_MODEL), F32),
        'w_r1': nrm(k[26], (DEPTH, D_MODEL, N_EXPERT_GROUPS), F32) * (D_MODEL ** -0.5),
        'b_r1': 0.01 * nrm(k[27], (DEPTH, N_EXPERT_GROUPS), F32),
        'w_r2': nrm(k[28], (DEPTH, D_MODEL, N_EXPERT_GROUPS, EXPERTS_PER_GROUP), F32) * (D_MODEL ** -0.5),
        'b_r2': 0.01 * nrm(k[29], (DEPTH, N_EXPERT_GROUPS, EXPERTS_PER_GROUP), F32),
        'w_gate': nrm(k[30], (DEPTH, N_EXPERTS, D_MODEL, D_EXPERT), F32) * (D_MODEL ** -0.5),
        'w_up': nrm(k[31], (DEPTH, N_EXPERTS, D_MODEL, D_EXPERT), F32) * (D_MODEL ** -0.5),
        'w_down': nrm(k[32], (DEPTH, N_EXPERTS, D_EXPERT, D_MODEL), F32) * (D_EXPERT ** -0.5) * DN_BETA,
        'ln2_g': 1.0 + 0.01 * nrm(k[33], (DEPTH, D_MODEL), F32),
        'ln2_b': 0.01 * nrm(k[34], (DEPTH, D_MODEL), F32),
    }


def reference(x_prompt, x_sample, cache_k, cache_v, state_ssm_re, state_ssm_im,
              w_in, ssm_a_re, ssm_a_im, ssm_log_dt, ssm_b_re, ssm_b_im, ssm_c_re, ssm_c_im, ssm_d,
              w_glu, b_glu, lambda_q1, lambda_k1, lambda_q2, lambda_k2, subln_g, rel_bias,
              w_out, ln1_g, ln1_b, w_r1, b_r1, w_r2, b_r2, w_gate, w_up, w_down, ln2_g, ln2_b):
    xp = x_prompt
    xs = x_sample
    bp, lp = xp.shape[:2]
    bs, ls = xs.shape[:2]
    past = cache_k.shape[2]
    q_pos_s = past + jnp.arange(ls, dtype=jnp.int32)
    k_pos_s = jnp.arange(past + ls, dtype=jnp.int32)
    kp_l, vp_l, hre_p_l, him_p_l = [], [], [], []
    ks_l, vs_l, hre_s_l, him_s_l = [], [], [], []
    for l in range(DEPTH):
        lam_init = lambda_init(l)
        lam = (jnp.exp(jnp.sum(lambda_q1[l].astype(F32) * lambda_k1[l].astype(F32)))
               - jnp.exp(jnp.sum(lambda_q2[l].astype(F32) * lambda_k2[l].astype(F32))) + lam_init)
        disc = ssm_discretize(ssm_a_re[l], ssm_a_im[l], ssm_log_dt[l], ssm_b_re[l], ssm_b_im[l])
        c_re = ssm_c_re[l].astype(F32)
        c_im = ssm_c_im[l].astype(F32)
        d = ssm_d[l].astype(F32)
        ffn = (w_out[l], ln1_g[l], ln1_b[l], w_r1[l], b_r1[l], w_r2[l], b_r2[l],
               w_gate[l], w_up[l], w_down[l], ln2_g[l], ln2_b[l])

        up, qp, kp, vp = split_in(xp @ w_in[l])
        up = up.astype(F32)
        yp_ssm, hp_re, hp_im = ssm_prompt(up, disc, c_re, c_im, d)
        op = diff_attn_prompt(qp, kp, vp, rel_bias, lam)
        xp_next = layer_finish(xp, ssm_glu(yp_ssm, w_glu[l], b_glu[l]), diff_attn_post(op, subln_g[l], lam_init), *ffn)

        us, qs, ks, vs = split_in(xs @ w_in[l])
        us = us.astype(F32)
        hs_re, hs_im = ssm_chunk(state_ssm_re[l].astype(F32), state_ssm_im[l].astype(F32), us, *disc)
        ys_ssm = ssm_readout(hs_re, hs_im, us, c_re, c_im, d)
        k_all = jnp.concatenate([cache_k[l].reshape(bs, past, N_ATTN_HEADS, 2, ATTN_HEAD_DIM).astype(ks.dtype), ks], axis=1)
        v_all = jnp.concatenate([cache_v[l].astype(vs.dtype), vs], axis=1)
        o_s = diff_attention(qs, k_all, v_all, q_pos_s, k_pos_s, rel_bias, lam)
        xs_next = layer_finish(xs, ssm_glu(ys_ssm, w_glu[l], b_glu[l]), diff_attn_post(o_s, subln_g[l], lam_init), *ffn)

        kp_l.append(kp.reshape(bp, lp, N_ATTN_HEADS, ATTN_QK_DIM).astype(cache_k.dtype))
        vp_l.append(vp.astype(cache_v.dtype))
        hre_p_l.append(hp_re.astype(state_ssm_re.dtype))
        him_p_l.append(hp_im.astype(state_ssm_im.dtype))
        ks_l.append(ks.reshape(bs, ls, N_ATTN_HEADS, ATTN_QK_DIM).astype(cache_k.dtype))
        vs_l.append(vs.astype(cache_v.dtype))
        hre_s_l.append(hs_re[:, -1].astype(state_ssm_re.dtype))
        him_s_l.append(hs_im[:, -1].astype(state_ssm_im.dtype))
        xp = xp_next
        xs = xs_next
    return (xp, xs, jnp.stack(kp_l), jnp.stack(vp_l), jnp.stack(hre_p_l), jnp.stack(him_p_l),
            jnp.stack(ks_l), jnp.stack(vs_l), jnp.stack(hre_s_l), jnp.stack(him_s_l))
```

```python
import functools
import math

import jax
import jax.numpy as jnp
from jax import lax
from jax.experimental import pallas as pl
from jax.experimental.pallas import tpu as pltpu

F32 = jnp.float32
BF16 = jnp.bfloat16

CHUNK = 64
SSM_GROUP_CH = 16
SSM_STATE = 64
N_ATTN_HEADS = 8
ATTN_HEAD_DIM = 64
REL_BUCKETS = 32
REL_MAX_DIST = 128
N_EXPERT_GROUPS = 4
EXPERTS_PER_GROUP = 8
N_EXPERTS = N_EXPERT_GROUPS * EXPERTS_PER_GROUP
DEPTH = 1
DN_ALPHA = (2 * DEPTH) ** 0.25
LN_EPS = 1e-5
RMS_EPS = 1e-5
NEG_INF = -1e30

LANES = 128
VMEM_LIMIT = 56 * 1024 * 1024

SSM_SUB = 16
SSM_OCT = LANES // SSM_GROUP_CH
ATTN_TILE = 256
MOE_ROWS = 256


def _cparams(sem, vmem=VMEM_LIMIT):
    return pltpu.CompilerParams(dimension_semantics=sem, vmem_limit_bytes=vmem)


def _layer_norm(r, g, b):
    mu = jnp.mean(r, axis=-1, keepdims=True)
    rc = r - mu
    var = jnp.mean(rc * rc, axis=-1, keepdims=True)
    return rc * lax.rsqrt(var + LN_EPS) * g + b


def _in_proj_kernel(x_ref, w_ref, u_ref, k_ref, v_ref, qb_ref, kb_ref, vb_ref, xb_scr, *, q_scale):
    j = pl.program_id(1)

    @pl.when(j == 0)
    def _():
        xb_scr[...] = x_ref[...].astype(BF16)

    z = jnp.dot(xb_scr[...], w_ref[...], preferred_element_type=F32)

    @pl.when(j == 0)
    def _():
        u_ref[...] = z

    @pl.when(j == 1)
    def _():
        qb_ref[...] = (z * q_scale).astype(BF16)

    @pl.when(j == 2)
    def _():
        k_ref[...] = z
        kb_ref[...] = z.astype(BF16)

    @pl.when(j == 3)
    def _():
        v_ref[...] = z
        vb_ref[...] = z.astype(BF16)


def _in_proj(x2d, w_in_b, tm):
    t, d = x2d.shape
    n = w_in_b.shape[1] // 4
    row = lambda i, j: (i, 0)
    f32_out = jax.ShapeDtypeStruct((t, n), F32)
    b16_out = jax.ShapeDtypeStruct((t, n), BF16)
    return pl.pallas_call(
        functools.partial(_in_proj_kernel, q_scale=ATTN_HEAD_DIM ** -0.5),
        grid=(t // tm, 4),
        in_specs=[pl.BlockSpec((tm, d), row), pl.BlockSpec((d, n), lambda i, j: (0, j))],
        out_specs=[pl.BlockSpec((tm, n), row)] * 6,
        out_shape=[f32_out, f32_out, f32_out, b16_out, b16_out, b16_out],
        scratch_shapes=[pltpu.VMEM((tm, d), BF16)],
        compiler_params=_cparams(("arbitrary", "arbitrary")),
    )(x2d, w_in_b)


def _ssm_tables(a_re, a_im, log_dt, b_re, b_im, c_re, c_im):
    hp = lax.Precision.HIGHEST
    g, p = a_re.shape
    c = b_re.shape[-1]
    s, o = SSM_SUB, SSM_OCT
    dt = jnp.exp(log_dt)[:, None]
    mag = jnp.exp(a_re * dt)
    ab_re = mag * jnp.cos(a_im * dt)
    ab_im = mag * jnp.sin(a_im * dt)
    nr = ab_re - 1.0
    ni = ab_im
    den = a_re * a_re + a_im * a_im
    z_re = ((nr * a_re + ni * a_im) / den)[..., None]
    z_im = ((ni * a_re - nr * a_im) / den)[..., None]
    bb_re = z_re * b_re - z_im * b_im
    bb_im = z_re * b_im + z_im * b_re
    n = jnp.arange(s + 1, dtype=F32)[:, None, None]
    pmag = jnp.exp(n * (a_re * dt))
    pw_re = pmag * jnp.cos(n * (a_im * dt))
    pw_im = pmag * jnp.sin(n * (a_im * dt))
    ca_re = c_re[None] * pw_re[:, :, None, :] - c_im[None] * pw_im[:, :, None, :]
    ca_im = c_re[None] * pw_im[:, :, None, :] + c_im[None] * pw_re[:, :, None, :]
    kern = (jnp.einsum('dgop,gpi->dgio', ca_re[:s], bb_re, precision=hp)
            - jnp.einsum('dgop,gpi->dgio', ca_im[:s], bb_im, precision=hp))
    jj = jnp.arange(s)
    lag = jj[None, :] - jj[:, None]
    kj = jnp.where((lag >= 0)[:, :, None, None, None], kern[jnp.clip(lag, 0, s - 1)], 0.0)
    kj = kj.reshape(s, s, g // o, o, c, c)
    eye = jnp.eye(o, dtype=F32)
    m = jnp.einsum('abGlic,lm->Galibmc', kj, eye)
    m = m.reshape(g // o, s * o * c, s * o * c).astype(BF16)
    nrev = (s - 1) - jnp.arange(s, dtype=F32)[:, None, None]
    rmag = jnp.exp(nrev * (a_re * dt))
    rv_re = rmag * jnp.cos(nrev * (a_im * dt))
    rv_im = rmag * jnp.sin(nrev * (a_im * dt))
    wb_re = rv_re[..., None] * bb_re[None] - rv_im[..., None] * bb_im[None]
    wb_im = rv_re[..., None] * bb_im[None] + rv_im[..., None] * bb_re[None]
    wb = jnp.stack([wb_re, wb_im], axis=0).reshape(2, s, g // o, o, p, c)
    w = jnp.einsum('rjGlpi,lm->Gjlirmp', wb, eye)
    w = w.reshape(g // o, s * o * c, 2 * o * p).astype(BF16)
    zc = jnp.stack([ca_re[1:], -ca_im[1:]], axis=0).reshape(2, s, g // o, o, c, p)
    z = jnp.einsum('rjGlop,lm->Grlpjmo', zc, eye)
    z = z.reshape(g // o, 2 * o * p, s * o * c).astype(BF16)
    a_s_re = pw_re[s].reshape(g // o, 1, o * p)
    a_s_im = pw_im[s].reshape(g // o, 1, o * p)
    return m, w, z, a_s_re, a_s_im


def _ssm_kernel(u_ref, m_ref, w_ref, z_ref, are_ref, aim_ref, d_ref, h0re_ref, h0im_ref,
                y_ref, hre_ref, him_ref, lhs_scr, v_scr, sin_scr, cre_scr, cim_scr, *, rows, cps):
    i = pl.program_id(1)
    half = cre_scr.shape[1]

    @pl.when(i == 0)
    def _():
        cre_scr[...] = jnp.zeros_like(cre_scr)
        cim_scr[...] = jnp.zeros_like(cim_scr)

    for j in range(SSM_SUB):
        piece = u_ref[pl.ds(j, rows, stride=SSM_SUB), :]
        lhs_scr[:, LANES * j:LANES * (j + 1)] = piece.astype(BF16)
    lhs = lhs_scr[...]
    v_scr[...] = jnp.dot(lhs, w_ref[0], preferred_element_type=F32)
    a_re = are_ref[0]
    a_im = aim_ref[0]

    def body(r, carry):
        s_re, s_im = carry
        idx = i * rows + r
        seq = idx // cps
        start = (idx % cps) == 0
        s_re = jnp.where(start, h0re_ref[pl.ds(seq, 1), :], s_re)
        s_im = jnp.where(start, h0im_ref[pl.ds(seq, 1), :], s_im)
        sin_scr[pl.ds(r, 1), 0:half] = s_re
        sin_scr[pl.ds(r, 1), half:2 * half] = s_im
        v_re = v_scr[pl.ds(r, 1), 0:half]
        v_im = v_scr[pl.ds(r, 1), half:2 * half]
        n_re = a_re * s_re - a_im * s_im + v_re
        n_im = a_re * s_im + a_im * s_re + v_im
        hre_ref[pl.ds(seq, 1), :] = n_re
        him_ref[pl.ds(seq, 1), :] = n_im
        return n_re, n_im

    s_re, s_im = lax.fori_loop(0, rows, body, (cre_scr[...], cim_scr[...]))
    cre_scr[...] = s_re
    cim_scr[...] = s_im

    y = (jnp.dot(lhs, m_ref[0], preferred_element_type=F32)
         + jnp.dot(sin_scr[...].astype(BF16), z_ref[0], preferred_element_type=F32))
    d = d_ref[...]
    for j in range(SSM_SUB):
        uj = u_ref[pl.ds(j, rows, stride=SSM_SUB), :]
        y_ref[pl.ds(j, rows, stride=SSM_SUB), :] = y[:, LANES * j:LANES * (j + 1)] + d * uj


def _ssm(u, tables, d_row, h0_re, h0_im, seq_len, tm):
    m, w, z, a_re, a_im = tables
    t, width = u.shape
    n_oct = width // LANES
    nseq = h0_re.shape[0]
    rows = tm // SSM_SUB
    half = a_re.shape[-1]
    kdim = SSM_SUB * LANES
    once = pl.Buffered(1)
    col = lambda g, i: (i, g)
    per_oct3 = lambda g, i: (g, 0, 0)
    per_oct2 = lambda g, i: (0, g)
    return pl.pallas_call(
        functools.partial(_ssm_kernel, rows=rows, cps=seq_len // SSM_SUB),
        grid=(n_oct, t // tm),
        in_specs=[
            pl.BlockSpec((tm, LANES), col),
            pl.BlockSpec((1, kdim, kdim), per_oct3, pipeline_mode=once),
            pl.BlockSpec((1, kdim, 2 * half), per_oct3, pipeline_mode=once),
            pl.BlockSpec((1, 2 * half, kdim), per_oct3, pipeline_mode=once),
            pl.BlockSpec((1, 1, half), per_oct3),
            pl.BlockSpec((1, 1, half), per_oct3),
            pl.BlockSpec((1, LANES), per_oct2),
            pl.BlockSpec((nseq, half), per_oct2),
            pl.BlockSpec((nseq, half), per_oct2),
        ],
        out_specs=[
            pl.BlockSpec((tm, LANES), col),
            pl.BlockSpec((nseq, half), per_oct2),
            pl.BlockSpec((nseq, half), per_oct2),
        ],
        out_shape=[
            jax.ShapeDtypeStruct((t, width), F32),
            jax.ShapeDtypeStruct((nseq, n_oct * half), F32),
            jax.ShapeDtypeStruct((nseq, n_oct * half), F32),
        ],
        scratch_shapes=[
            pltpu.VMEM((rows, kdim), BF16),
            pltpu.VMEM((rows, 2 * half), F32),
            pltpu.VMEM((rows, 2 * half), F32),
            pltpu.VMEM((1, half), F32),
            pltpu.VMEM((1, half), F32),
        ],
        compiler_params=_cparams(("arbitrary", "arbitrary")),
    )(u, m, w, z, a_re, a_im, d_row, h0_re, h0_im)


def _rel_bucket(rel):
    nb = REL_BUCKETS // 2
    max_exact = nb // 2
    n = jnp.abs(rel)
    nf = jnp.maximum(n, 1).astype(F32)
    large = max_exact + (jnp.log(nf / max_exact) / math.log(REL_MAX_DIST / max_exact)
                         * (nb - max_exact)).astype(jnp.int32)
    large = jnp.minimum(large, nb - 1)
    return jnp.where(rel > 0, nb, 0) + jnp.where(n < max_exact, n, large)


def _bias_table(rel_bias, qpos, kpos):
    bias = jnp.transpose(rel_bias.astype(F32)[_rel_bucket(kpos[None, :] - qpos[:, None])], (2, 0, 1))
    mask = (kpos[None, :] // CHUNK) <= (qpos[:, None] // CHUNK)
    return jnp.where(mask[None], bias, NEG_INF)


def _split_heads_lhs(q):
    lane = lax.broadcasted_iota(jnp.int32, q.shape, 1)
    zero = jnp.zeros_like(q)
    return jnp.concatenate([jnp.where(lane < ATTN_HEAD_DIM, q, zero),
                            jnp.where(lane >= ATTN_HEAD_DIM, q, zero)], axis=0)


def _nt_dot(a, b):
    return lax.dot_general(a, b, (((1,), (1,)), ((), ())), preferred_element_type=F32)


def _diff_finish(acc, l, lam, g, tq, post_scale):
    o = acc[:tq] / l[:tq] - lam * (acc[tq:] / l[tq:])
    ms = jnp.mean(o * o, axis=-1, keepdims=True)
    return o * lax.rsqrt(ms + RMS_EPS) * g * post_scale


def _flash_kernel(lam_ref, q_ref, k_ref, v_ref, bias_ref, g_ref, o_ref,
                  m_scr, l_scr, acc_scr, *, tq, post_scale):
    qi = pl.program_id(2)
    lhs = _split_heads_lhs(q_ref[...])
    m_scr[...] = jnp.full_like(m_scr, NEG_INF)
    l_scr[...] = jnp.zeros_like(l_scr)
    acc_scr[...] = jnp.zeros_like(acc_scr)

    def step(off, bias):
        k = k_ref[pl.ds(off, tq), :]
        v = v_ref[pl.ds(off, tq), :]
        s = _nt_dot(lhs, k)
        if bias is not None:
            s = s + jnp.concatenate([bias, bias], axis=0)
        m_prev = m_scr[...]
        m_new = jnp.maximum(m_prev, jnp.max(s, axis=1, keepdims=True))
        alpha = jnp.exp(m_prev - m_new)
        p = jnp.exp(s - m_new)
        l_scr[...] = alpha * l_scr[...] + jnp.sum(p, axis=1, keepdims=True)
        acc_scr[...] = alpha * acc_scr[...] + jnp.dot(p.astype(BF16), v, preferred_element_type=F32)
        m_scr[...] = m_new

    def far(ki, carry):
        step(pl.multiple_of(ki * tq, tq), None)
        return carry

    lax.fori_loop(0, jnp.maximum(qi - 1, 0), far, 0)

    @pl.when(qi > 0)
    def _():
        step(pl.multiple_of((qi - 1) * tq, tq), bias_ref[0, 0])

    step(pl.multiple_of(qi * tq, tq), bias_ref[0, 1])
    o = _diff_finish(acc_scr[...], l_scr[...], lam_ref[0], g_ref[...], tq, post_scale)
    o_ref[...] = o.astype(o_ref.dtype)


def _attn_prompt(qb, kb, vb, rel_bias, lam, g_row, batch, seq_len, post_scale, tq):
    assert tq >= REL_MAX_DIST and tq % CHUNK == 0
    t, width = qb.shape
    nh = width // LANES
    nq = seq_len // tq
    pos = jnp.arange(2 * tq, dtype=jnp.int32)
    table = _bias_table(rel_bias, pos[tq:], pos)
    far_bias = rel_bias.astype(F32)[_rel_bucket(jnp.int32(-(tq + 1)))]
    table = table - far_bias[:, None, None]
    table = jnp.stack([table[:, :, :tq], table[:, :, tq:]], axis=1)
    qmap = lambda b, h, qi: (b * nq + qi, h)
    kvmap = lambda b, h, qi: (b, h)
    return pl.pallas_call(
        functools.partial(_flash_kernel, tq=tq, post_scale=post_scale),
        grid=(batch, nh, nq),
        in_specs=[
            pl.BlockSpec(memory_space=pltpu.SMEM),
            pl.BlockSpec((tq, LANES), qmap),
            pl.BlockSpec((seq_len, LANES), kvmap),
            pl.BlockSpec((seq_len, LANES), kvmap),
            pl.BlockSpec((1, 2, tq, tq), lambda b, h, qi: (h, 0, 0, 0)),
            pl.BlockSpec((1, LANES), lambda b, h, qi: (0, 0)),
        ],
        out_specs=pl.BlockSpec((tq, LANES), qmap),
        out_shape=jax.ShapeDtypeStruct((t, width), BF16),
        scratch_shapes=[
            pltpu.VMEM((2 * tq, 1), F32),
            pltpu.VMEM((2 * tq, 1), F32),
            pltpu.VMEM((2 * tq, LANES), F32),
        ],
        compiler_params=_cparams(("arbitrary", "arbitrary", "arbitrary")),
    )(lam, qb, kb, vb, table, g_row)


def _sample_attn_kernel(lam_ref, q_ref, ck_ref, cv_ref, kn_ref, vn_ref, bias_ref, g_ref, o_ref,
                        *, past, post_scale):
    tq = q_ref.shape[0]
    lhs = _split_heads_lhs(q_ref[...])
    bias = bias_ref[0]
    bias2 = jnp.concatenate([bias, bias], axis=0)
    s_c = _nt_dot(lhs, ck_ref[0].astype(BF16)) + bias2[:, :past]
    s_n = _nt_dot(lhs, kn_ref[...]) + bias2[:, past:]
    m = jnp.maximum(jnp.max(s_c, axis=1, keepdims=True), jnp.max(s_n, axis=1, keepdims=True))
    p_c = jnp.exp(s_c - m)
    p_n = jnp.exp(s_n - m)
    l = jnp.sum(p_c, axis=1, keepdims=True) + jnp.sum(p_n, axis=1, keepdims=True)
    acc = (jnp.dot(p_c.astype(BF16), cv_ref[0].astype(BF16), preferred_element_type=F32)
           + jnp.dot(p_n.astype(BF16), vn_ref[...], preferred_element_type=F32))
    o = _diff_finish(acc, l, lam_ref[0], g_ref[...], tq, post_scale)
    o_ref[...] = o.astype(o_ref.dtype)


def _attn_sample(qb, kb, vb, cache_k, cache_v, rel_bias, lam, g_row, post_scale):
    nstream, past, width = cache_k.shape
    nh = width // LANES
    ls = qb.shape[0] // nstream
    q_pos = past + jnp.arange(ls, dtype=jnp.int32)
    k_pos = jnp.arange(past + ls, dtype=jnp.int32)
    table = _bias_table(rel_bias, q_pos, k_pos)
    new = lambda b, h: (b, h)
    old = lambda b, h: (b, 0, h)
    return pl.pallas_call(
        functools.partial(_sample_attn_kernel, past=past, post_scale=post_scale),
        grid=(nstream, nh),
        in_specs=[
            pl.BlockSpec(memory_space=pltpu.SMEM),
            pl.BlockSpec((ls, LANES), new),
            pl.BlockSpec((1, past, LANES), old),
            pl.BlockSpec((1, past, LANES), old),
            pl.BlockSpec((ls, LANES), new),
            pl.BlockSpec((ls, LANES), new),
            pl.BlockSpec((1, ls, past + ls), lambda b, h: (h, 0, 0)),
            pl.BlockSpec((1, LANES), lambda b, h: (0, 0)),
        ],
        out_specs=pl.BlockSpec((ls, LANES), new),
        out_shape=jax.ShapeDtypeStruct(qb.shape, BF16),
        compiler_params=_cparams(("arbitrary", "arbitrary")),
    )(lam, qb, cache_k, cache_v, kb, vb, table, g_row)


def _mix_kernel(x_ref, y_ref, a_ref, wglu_ref, bglu_ref, wout_ref, g1_ref, b1_ref,
                wrh_ref, wrl_ref, br_ref, h_ref, lg_ref):
    half = y_ref.shape[1]
    g = jax.nn.gelu(y_ref[...])
    t = jnp.dot(g.astype(BF16), wglu_ref[...], preferred_element_type=F32) + bglu_ref[...]
    so = g * jax.nn.sigmoid(t)
    mix = (jnp.dot(so.astype(BF16), wout_ref[0:half, :], preferred_element_type=F32)
           + jnp.dot(a_ref[...], wout_ref[half:2 * half, :], preferred_element_type=F32))
    h = _layer_norm(DN_ALPHA * x_ref[...] + mix, g1_ref[...], b1_ref[...])
    h_ref[...] = h
    h_hi = h.astype(BF16)
    h_lo = (h - h_hi.astype(F32)).astype(BF16)
    lg = (jnp.dot(h_hi, wrh_ref[...], preferred_element_type=F32)
          + jnp.dot(h_lo, wrh_ref[...], preferred_element_type=F32)
          + jnp.dot(h_hi, wrl_ref[...], preferred_element_type=F32))
    lg_ref[...] = lg + br_ref[...]


def _mix(x2d, y_ssm, attn, wglu_b, bglu, wout_b, g1, b1, wr_hi, wr_lo, br, tm):
    t, d = x2d.shape
    half = d // 2
    row = lambda i: (i, 0)
    fix = lambda i: (0, 0)
    once = pl.Buffered(1)
    return pl.pallas_call(
        _mix_kernel,
        grid=(t // tm,),
        in_specs=[
            pl.BlockSpec((tm, d), row),
            pl.BlockSpec((tm, half), row),
            pl.BlockSpec((tm, half), row),
            pl.BlockSpec((half, half), fix, pipeline_mode=once),
            pl.BlockSpec((1, half), fix),
            pl.BlockSpec((d, d), fix, pipeline_mode=once),
            pl.BlockSpec((1, d), fix),
            pl.BlockSpec((1, d), fix),
            pl.BlockSpec((d, LANES), fix),
            pl.BlockSpec((d, LANES), fix),
            pl.BlockSpec((1, LANES), fix),
        ],
        out_specs=[pl.BlockSpec((tm, d), row), pl.BlockSpec((tm, LANES), row)],
        out_shape=[jax.ShapeDtypeStruct((t, d), F32), jax.ShapeDtypeStruct((t, LANES), F32)],
        compiler_params=_cparams(("arbitrary",)),
    )(x2d, y_ssm, attn, wglu_b, bglu, wout_b, g1, b1, wr_hi, wr_lo, br)


def _route_kernel(lg_ref, eid_ref, gate_ref, rank_ref, cnt_ref):
    i = pl.program_id(0)
    tm = lg_ref.shape[0]
    ng = N_EXPERT_GROUPS

    @pl.when(i == 0)
    def _():
        cnt_ref[...] = jnp.zeros_like(cnt_ref)

    lg = lg_ref[...]
    lane = lax.broadcasted_iota(jnp.int32, lg.shape, 1)
    is_g = lane < ng
    l1 = jnp.where(is_g, lg, -jnp.inf)
    m1 = jnp.max(l1, axis=1, keepdims=True)
    z1 = jnp.sum(jnp.where(is_g, jnp.exp(l1 - m1), 0.0), axis=1, keepdims=True)
    w_grp = 1.0 / z1
    el = (lane - ng).astype(F32)
    none = float(LANES)
    grp = jnp.min(jnp.where(l1 == m1, lane.astype(F32), none), axis=1, keepdims=True)
    lo = grp * EXPERTS_PER_GROUP
    in_grp = jnp.where(el >= lo, jnp.where(el < lo + EXPERTS_PER_GROUP, 1.0, 0.0), 0.0) > 0.5
    l2 = jnp.where(in_grp, lg, -jnp.inf)
    t1 = jnp.max(l2, axis=1, keepdims=True)
    i1 = jnp.min(jnp.where(l2 == t1, el, none), axis=1, keepdims=True)
    l2b = jnp.where(el == i1, -jnp.inf, l2)
    t2 = jnp.max(l2b, axis=1, keepdims=True)
    i2 = jnp.min(jnp.where(l2b == t2, el, none), axis=1, keepdims=True)
    e2 = jnp.exp(t2 - t1)
    den = 1.0 + e2
    g0 = w_grp * (1.0 / den)
    g1 = w_grp * (e2 / den)
    hit1 = el == i1
    hit2 = el == i2
    oh = jnp.where(hit1, 1.0, jnp.where(hit2, 1.0, 0.0))
    r_id = lax.broadcasted_iota(jnp.int32, (tm, tm), 0)
    c_id = lax.broadcasted_iota(jnp.int32, (tm, tm), 1)
    tri = jnp.where(c_id < r_id, 1.0, 0.0).astype(BF16)
    before = jnp.dot(tri, oh.astype(BF16), preferred_element_type=F32) + cnt_ref[...]
    rank1 = jnp.sum(jnp.where(hit1, before, 0.0), axis=1, keepdims=True).astype(jnp.int32)
    rank2 = jnp.sum(jnp.where(hit2, before, 0.0), axis=1, keepdims=True).astype(jnp.int32)
    cnt_ref[...] = cnt_ref[...] + jnp.sum(oh, axis=0, keepdims=True)
    eid_ref[...] = jnp.where(lane == 0, i1, jnp.where(lane == 1, i2, 0.0)).astype(jnp.int32)
    gate_ref[...] = jnp.where(lane == 0, g0, jnp.where(lane == 1, g1, 0.0))
    rank_ref[...] = jnp.where(lane == 0, rank1, jnp.where(lane == 1, rank2, 0))


def _route(logits, tm):
    t = logits.shape[0]
    row = lambda i: (i, 0)
    return pl.pallas_call(
        _route_kernel,
        grid=(t // tm,),
        in_specs=[pl.BlockSpec((tm, LANES), row)],
        out_specs=[pl.BlockSpec((tm, LANES), row)] * 3 + [pl.BlockSpec((1, LANES), lambda i: (0, 0))],
        out_shape=[
            jax.ShapeDtypeStruct((t, LANES), jnp.int32),
            jax.ShapeDtypeStruct((t, LANES), F32),
            jax.ShapeDtypeStruct((t, LANES), jnp.int32),
            jax.ShapeDtypeStruct((1, LANES), F32),
        ],
        compiler_params=_cparams(("arbitrary",)),
    )(logits)


def _row_copy(src, s_row, dst, d_row, sem):
    return pltpu.make_async_copy(src.at[pl.ds(s_row, 1), :], dst.at[pl.ds(d_row, 1), :], sem)


def _dispatch_kernel(dest_ref, h_ref, xin_ref, xbuf_ref, sem):
    del xin_ref
    tm = h_ref.shape[0]

    def issue(r, c):
        _row_copy(h_ref, r, xbuf_ref, dest_ref[0, 0, 2 * r], sem).start()
        _row_copy(h_ref, r, xbuf_ref, dest_ref[0, 0, 2 * r + 1], sem).start()
        return c

    lax.fori_loop(0, tm, issue, 0)

    def drain(r, c):
        _row_copy(h_ref, 0, xbuf_ref, 0, sem).wait()
        _row_copy(h_ref, 0, xbuf_ref, 0, sem).wait()
        return c

    lax.fori_loop(0, tm, drain, 0)


def _dispatch(h, dest3, n_rows, tm):
    t, d = h.shape
    xbuf0 = jnp.zeros((n_rows, d), h.dtype)
    return pl.pallas_call(
        _dispatch_kernel,
        grid=(t // tm,),
        in_specs=[
            pl.BlockSpec((1, 1, 2 * tm), lambda i: (i, 0, 0), memory_space=pltpu.SMEM),
            pl.BlockSpec((tm, d), lambda i: (i, 0)),
            pl.BlockSpec(memory_space=pl.ANY),
        ],
        out_specs=pl.BlockSpec(memory_space=pl.ANY),
        out_shape=jax.ShapeDtypeStruct((n_rows, d), h.dtype),
        scratch_shapes=[pltpu.SemaphoreType.DMA(())],
        input_output_aliases={2: 0},
        compiler_params=_cparams(("arbitrary",)),
    )(dest3, h, xbuf0)


def _expert_kernel(be_ref, nused_ref, x_ref, wg_ref, wu_ref, wd_ref, y_ref):
    i = pl.program_id(0)

    @pl.when(i < nused_ref[0])
    def _():
        x = x_ref[...].astype(BF16)
        gt = jnp.dot(x, wg_ref[0], preferred_element_type=F32)
        up = jnp.dot(x, wu_ref[0], preferred_element_type=F32)
        a = (jax.nn.silu(gt) * up).astype(BF16)
        y_ref[...] = jnp.dot(a, wd_ref[0], preferred_element_type=F32)

    @pl.when(i >= nused_ref[0])
    def _():
        y_ref[...] = jnp.zeros_like(y_ref)


def _experts(xbuf, block_e, n_used, wg_b, wu_b, wd_b):
    n_rows, d = xbuf.shape
    de = wg_b.shape[-1]
    nb = n_rows // MOE_ROWS
    wmap = lambda i, be, nu: (be[i], 0, 0)
    grid_spec = pltpu.PrefetchScalarGridSpec(
        num_scalar_prefetch=2,
        grid=(nb,),
        in_specs=[
            pl.BlockSpec((MOE_ROWS, d), lambda i, be, nu: (i, 0)),
            pl.BlockSpec((1, d, de), wmap),
            pl.BlockSpec((1, d, de), wmap),
            pl.BlockSpec((1, de, d), wmap),
        ],
        out_specs=pl.BlockSpec((MOE_ROWS, d), lambda i, be, nu: (i, 0)),
    )
    return pl.pallas_call(
        _expert_kernel,
        grid_spec=grid_spec,
        out_shape=jax.ShapeDtypeStruct((n_rows, d), F32),
        compiler_params=_cparams(("arbitrary",)),
    )(block_e, n_used, xbuf, wg_b, wu_b, wd_b)


def _combine_kernel(dest_ref, h_ref, gate_ref, g2_ref, b2_ref, ybuf_ref, o_ref, ya_scr, yb_scr, sem):
    tm = h_ref.shape[0]

    def issue(r, c):
        _row_copy(ybuf_ref, dest_ref[0, 0, 2 * r], ya_scr, r, sem).start()
        _row_copy(ybuf_ref, dest_ref[0, 0, 2 * r + 1], yb_scr, r, sem).start()
        return c

    lax.fori_loop(0, tm, issue, 0)

    def drain(r, c):
        _row_copy(ybuf_ref, 0, ya_scr, 0, sem).wait()
        _row_copy(ybuf_ref, 0, yb_scr, 0, sem).wait()
        return c

    lax.fori_loop(0, tm, drain, 0)
    gate = gate_ref[...]
    f = ya_scr[...] * gate[:, 0:1] + yb_scr[...] * gate[:, 1:2]
    o_ref[...] = _layer_norm(DN_ALPHA * h_ref[...] + f, g2_ref[...], b2_ref[...])


def _combine(h, gate, dest3, ybuf, g2, b2, tm):
    t, d = h.shape
    row = lambda i: (i, 0)
    fix = lambda i: (0, 0)
    return pl.pallas_call(
        _combine_kernel,
        grid=(t // tm,),
        in_specs=[
            pl.BlockSpec((1, 1, 2 * tm), lambda i: (i, 0, 0), memory_space=pltpu.SMEM),
            pl.BlockSpec((tm, d), row),
            pl.BlockSpec((tm, LANES), row),
            pl.BlockSpec((1, d), fix),
            pl.BlockSpec((1, d), fix),
            pl.BlockSpec(memory_space=pl.ANY),
        ],
        out_specs=pl.BlockSpec((tm, d), row),
        out_shape=jax.ShapeDtypeStruct((t, d), F32),
        scratch_shapes=[pltpu.VMEM((tm, d), F32), pltpu.VMEM((tm, d), F32), pltpu.SemaphoreType.DMA(())],
        compiler_params=_cparams(("arbitrary",)),
    )(dest3, h, gate, g2, b2, ybuf)


def _moe_finish(h, logits, wg_b, wu_b, wd_b, g2, b2, tm_route, tm_rows):
    t = h.shape[0]
    eid, gate, rank, cnt = _route(logits, tm_route)
    counts = cnt[0, N_EXPERT_GROUPS:N_EXPERT_GROUPS + N_EXPERTS].astype(jnp.int32)
    padded = (counts + MOE_ROWS - 1) // MOE_ROWS * MOE_ROWS
    pad_end = jnp.cumsum(padded)
    pad_start = pad_end - padded
    dest = pad_start[eid[:, :2]] + rank[:, :2]
    n_blocks = -(-(2 * t) // MOE_ROWS) + N_EXPERTS
    block_e = jnp.minimum(jnp.searchsorted(pad_end, jnp.arange(n_blocks) * MOE_ROWS, side='right'),
                          N_EXPERTS - 1).astype(jnp.int32)
    n_used = (pad_end[-1:] // MOE_ROWS).astype(jnp.int32)
    dest3 = dest.astype(jnp.int32).reshape(t // tm_rows, 1, 2 * tm_rows)
    xbuf = _dispatch(h, dest3, n_blocks * MOE_ROWS, tm_rows)
    ybuf = _experts(xbuf, block_e, n_used, wg_b, wu_b, wd_b)
    return _combine(h, gate, dest3, ybuf, g2, b2, tm_rows)


def _pick(t, pref):
    tm = min(pref, t)
    while t % tm:
        tm //= 2
    return tm


def _stream(x, h0_re, h0_im, cache_k, cache_v, params, lam, lam_init):
    (w_in_b, tables, d_row, wglu_b, bglu, rel_bias, g_row, wout_b, g1, b1,
     wr_hi, wr_lo, br, wg_b, wu_b, wd_b, g2, b2) = params
    b, l, d = x.shape
    t = b * l
    x2d = x.reshape(t, d)
    u, k, v, qb, kb, vb = _in_proj(x2d, w_in_b, _pick(t, 512))
    y_ssm, h_re, h_im = _ssm(u, tables, d_row, h0_re, h0_im, l, _pick(t, 4096))
    post = 1.0 - lam_init
    if cache_k is None:
        attn = _attn_prompt(qb, kb, vb, rel_bias, lam, g_row, b, l, post, ATTN_TILE)
    else:
        attn = _attn_sample(qb, kb, vb, cache_k, cache_v, rel_bias, lam, g_row, post)
    h, logits = _mix(x2d, y_ssm, attn, wglu_b, bglu, wout_b, g1, b1, wr_hi, wr_lo, br, _pick(t, 256))
    out = _moe_finish(h, logits, wg_b, wu_b, wd_b, g2, b2, _pick(t, 512), _pick(t, 256))
    return out.reshape(b, l, d), k, v, h_re, h_im


def kernel(x_prompt, x_sample, cache_k, cache_v, state_ssm_re, state_ssm_im, w_in, ssm_a_re, ssm_a_im, ssm_log_dt, ssm_b_re, ssm_b_im, ssm_c_re, ssm_c_im, ssm_d, w_glu, b_glu, lambda_q1, lambda_k1, lambda_q2, lambda_k2, subln_g, rel_bias, w_out, ln1_g, ln1_b, w_r1, b_r1, w_r2, b_r2, w_gate, w_up, w_down, ln2_g, ln2_b):
    assert w_in.shape[0] == DEPTH
    bp, lp, d = x_prompt.shape
    bs, ls, _ = x_sample.shape
    past = cache_k.shape[2]
    nh, dqk = cache_k.shape[3], cache_k.shape[4]
    n_grp, n_state = state_ssm_re.shape[2], state_ssm_re.shape[3]
    l = 0
    lam_init = 0.8 - 0.6 * math.exp(-0.3 * l)
    lam = (jnp.exp(jnp.sum(lambda_q1[l].astype(F32) * lambda_k1[l].astype(F32)))
           - jnp.exp(jnp.sum(lambda_q2[l].astype(F32) * lambda_k2[l].astype(F32))) + lam_init).reshape(1)
    tables = _ssm_tables(ssm_a_re[l].astype(F32), ssm_a_im[l].astype(F32), ssm_log_dt[l].astype(F32),
                         ssm_b_re[l].astype(F32), ssm_b_im[l].astype(F32),
                         ssm_c_re[l].astype(F32), ssm_c_im[l].astype(F32))
    wr = jnp.concatenate([w_r1[l].astype(F32), w_r2[l].astype(F32).reshape(d, N_EXPERTS)], axis=1)
    wr = jnp.pad(wr, ((0, 0), (0, LANES - wr.shape[1])))
    wr_hi = wr.astype(BF16)
    wr_lo = (wr - wr_hi.astype(F32)).astype(BF16)
    br = jnp.concatenate([b_r1[l].astype(F32), b_r2[l].astype(F32).reshape(N_EXPERTS)])
    br = jnp.pad(br, (0, LANES - br.shape[0])).reshape(1, LANES)
    params = (
        w_in[l].astype(BF16), tables, ssm_d[l].astype(F32).reshape(1, -1),
        w_glu[l].astype(BF16), b_glu[l].astype(F32).reshape(1, -1), rel_bias,
        subln_g[l].astype(F32).reshape(1, -1), w_out[l].astype(BF16),
        ln1_g[l].astype(F32).reshape(1, -1), ln1_b[l].astype(F32).reshape(1, -1),
        wr_hi, wr_lo, br,
        w_gate[l].astype(BF16), w_up[l].astype(BF16), w_down[l].astype(BF16),
        ln2_g[l].astype(F32).reshape(1, -1), ln2_b[l].astype(F32).reshape(1, -1),
    )
    zeros = jnp.zeros((bp, n_grp * n_state), F32)
    yp, kp, vp, hp_re, hp_im = _stream(x_prompt, zeros, zeros, None, None, params, lam, lam_init)
    ys, ks, vs, hs_re, hs_im = _stream(
        x_sample, state_ssm_re[l].astype(F32).reshape(bs, -1), state_ssm_im[l].astype(F32).reshape(bs, -1),
        cache_k[l].reshape(bs, past, nh * dqk), cache_v[l].reshape(bs, past, -1), params, lam, lam_init)
    return (yp, ys,
            kp.reshape(1, bp, lp, nh, dqk).astype(cache_k.dtype),
            vp.reshape(1, bp, lp, nh, -1).astype(cache_v.dtype),
            hp_re.reshape(1, bp, n_grp, n_state).astype(state_ssm_re.dtype),
            hp_im.reshape(1, bp, n_grp, n_state).astype(state_ssm_im.dtype),
            ks.reshape(1, bs, ls, nh, dqk).astype(cache_k.dtype),
            vs.reshape(1, bs, ls, nh, -1).astype(cache_v.dtype),
            hs_re.reshape(1, bs, n_grp, n_state).astype(state_ssm_re.dtype),
            hs_im.reshape(1, bs, n_grp, n_state).astype(state_ssm_im.dtype))
```

```python
import functools
import math

import jax
import jax.numpy as jnp
from jax import lax
from jax.experimental import pallas as pl
from jax.experimental.pallas import tpu as pltpu

F32 = jnp.float32
BF16 = jnp.bfloat16

CHUNK = 64
SSM_GROUP_CH = 16
SSM_STATE = 64
N_ATTN_HEADS = 8
ATTN_HEAD_DIM = 64
REL_BUCKETS = 32
REL_MAX_DIST = 128
N_EXPERT_GROUPS = 4
EXPERTS_PER_GROUP = 8
N_EXPERTS = N_EXPERT_GROUPS * EXPERTS_PER_GROUP
DEPTH = 1
DN_ALPHA = (2 * DEPTH) ** 0.25
LN_EPS = 1e-5
RMS_EPS = 1e-5
NEG_INF = -1e30
LOG2E = math.log2(math.e)

LANES = 128
VMEM_LIMIT = 56 * 1024 * 1024

SSM_SUB = 16
SSM_OCT = LANES // SSM_GROUP_CH
ATTN_TILE = 1024
ATTN_COLS = 256
MOE_ROWS = 256


def _cparams(sem, vmem=VMEM_LIMIT):
    return pltpu.CompilerParams(dimension_semantics=sem, vmem_limit_bytes=vmem)


def _layer_norm(r, g, b):
    mu = jnp.mean(r, axis=-1, keepdims=True)
    rc = r - mu
    var = jnp.mean(rc * rc, axis=-1, keepdims=True)
    return rc * lax.rsqrt(var + LN_EPS) * g + b


def _store_heads_transposed(dst_ref, z):
    for h in range(dst_ref.shape[0]):
        dst_ref[h] = z[:, LANES * h:LANES * (h + 1)].T.astype(BF16)


def _store_heads_split(dst_ref, z):
    for h in range(dst_ref.shape[1]):
        dst_ref[:, h, :] = z[:, LANES * h:LANES * (h + 1)]


def _in_proj_kernel(x_ref, w_ref, u_ref, k_ref, v_ref, qb_ref, kb_ref, vb_ref, qt_ref, vt_ref, xb_scr,
                    *, q_scale):
    j = pl.program_id(1)

    @pl.when(j == 0)
    def _():
        xb_scr[...] = x_ref[...].astype(BF16)

    z = jnp.dot(xb_scr[...], w_ref[...], preferred_element_type=F32)

    @pl.when(j == 0)
    def _():
        u_ref[...] = z

    @pl.when(j == 1)
    def _():
        zq = z * q_scale
        qb_ref[...] = zq.astype(BF16)
        _store_heads_transposed(qt_ref, zq * LOG2E)

    @pl.when(j == 2)
    def _():
        _store_heads_split(k_ref, z)
        kb_ref[...] = z.astype(BF16)

    @pl.when(j == 3)
    def _():
        _store_heads_split(v_ref, z)
        vb_ref[...] = z.astype(BF16)
        _store_heads_transposed(vt_ref, z)


def _in_proj(x2d, w_in_b, tm):
    t, d = x2d.shape
    n = w_in_b.shape[1] // 4
    nh = n // LANES
    row = lambda i, j: (i, 0)
    f32_out = jax.ShapeDtypeStruct((t, n), F32)
    b16_out = jax.ShapeDtypeStruct((t, n), BF16)
    tr_out = jax.ShapeDtypeStruct((nh, LANES, t), BF16)
    tr_spec = pl.BlockSpec((nh, LANES, tm), lambda i, j: (0, 0, i))
    cache_out = jax.ShapeDtypeStruct((t, nh, LANES), F32)
    cache_spec = pl.BlockSpec((tm, nh, LANES), lambda i, j: (i, 0, 0))
    return pl.pallas_call(
        functools.partial(_in_proj_kernel, q_scale=ATTN_HEAD_DIM ** -0.5),
        grid=(t // tm, 4),
        in_specs=[pl.BlockSpec((tm, d), row), pl.BlockSpec((d, n), lambda i, j: (0, j))],
        out_specs=[pl.BlockSpec((tm, n), row), cache_spec, cache_spec] + [pl.BlockSpec((tm, n), row)] * 3
        + [tr_spec] * 2,
        out_shape=[f32_out, cache_out, cache_out, b16_out, b16_out, b16_out, tr_out, tr_out],
        scratch_shapes=[pltpu.VMEM((tm, d), BF16)],
        compiler_params=_cparams(("arbitrary", "arbitrary")),
    )(x2d, w_in_b)


def _ssm_tables(a_re, a_im, log_dt, b_re, b_im, c_re, c_im):
    hp = lax.Precision.HIGHEST
    g, p = a_re.shape
    c = b_re.shape[-1]
    s, o = SSM_SUB, SSM_OCT
    dt = jnp.exp(log_dt)[:, None]
    mag = jnp.exp(a_re * dt)
    ab_re = mag * jnp.cos(a_im * dt)
    ab_im = mag * jnp.sin(a_im * dt)
    nr = ab_re - 1.0
    ni = ab_im
    den = a_re * a_re + a_im * a_im
    z_re = ((nr * a_re + ni * a_im) / den)[..., None]
    z_im = ((ni * a_re - nr * a_im) / den)[..., None]
    bb_re = z_re * b_re - z_im * b_im
    bb_im = z_re * b_im + z_im * b_re
    n = jnp.arange(s + 1, dtype=F32)[:, None, None]
    pmag = jnp.exp(n * (a_re * dt))
    pw_re = pmag * jnp.cos(n * (a_im * dt))
    pw_im = pmag * jnp.sin(n * (a_im * dt))
    ca_re = c_re[None] * pw_re[:, :, None, :] - c_im[None] * pw_im[:, :, None, :]
    ca_im = c_re[None] * pw_im[:, :, None, :] + c_im[None] * pw_re[:, :, None, :]
    kern = (jnp.einsum('dgop,gpi->dgio', ca_re[:s], bb_re, precision=hp)
            - jnp.einsum('dgop,gpi->dgio', ca_im[:s], bb_im, precision=hp))
    jj = jnp.arange(s)
    lag = jj[None, :] - jj[:, None]
    kj = jnp.where((lag >= 0)[:, :, None, None, None], kern[jnp.clip(lag, 0, s - 1)], 0.0)
    kj = kj.reshape(s, s, g // o, o, c, c)
    eye = jnp.eye(o, dtype=F32)
    m = jnp.einsum('abGlic,lm->Galibmc', kj, eye)
    m = m.reshape(g // o, s * o * c, s * o * c).astype(BF16)
    nrev = (s - 1) - jnp.arange(s, dtype=F32)[:, None, None]
    rmag = jnp.exp(nrev * (a_re * dt))
    rv_re = rmag * jnp.cos(nrev * (a_im * dt))
    rv_im = rmag * jnp.sin(nrev * (a_im * dt))
    wb_re = rv_re[..., None] * bb_re[None] - rv_im[..., None] * bb_im[None]
    wb_im = rv_re[..., None] * bb_im[None] + rv_im[..., None] * bb_re[None]
    wb = jnp.stack([wb_re, wb_im], axis=0).reshape(2, s, g // o, o, p, c)
    w = jnp.einsum('rjGlpi,lm->Gjlirmp', wb, eye)
    w = w.reshape(g // o, s * o * c, 2 * o * p).astype(BF16)
    zc = jnp.stack([ca_re[1:], -ca_im[1:]], axis=0).reshape(2, s, g // o, o, c, p)
    z = jnp.einsum('rjGlop,lm->Grlpjmo', zc, eye)
    z = z.reshape(g // o, 2 * o * p, s * o * c).astype(BF16)
    a_s_re = pw_re[s].reshape(g // o, 1, o * p)
    a_s_im = pw_im[s].reshape(g // o, 1, o * p)
    return m, w, z, a_s_re, a_s_im


def _ssm_kernel(u_ref, m_ref, w_ref, z_ref, are_ref, aim_ref, d_ref, h0re_ref, h0im_ref,
                y_ref, hre_ref, him_ref, lhs_scr, v_scr, sin_scr, cre_scr, cim_scr, *, rows, cps):
    i = pl.program_id(1)
    half = cre_scr.shape[1]

    @pl.when(i == 0)
    def _():
        cre_scr[...] = jnp.zeros_like(cre_scr)
        cim_scr[...] = jnp.zeros_like(cim_scr)

    for j in range(SSM_SUB):
        piece = u_ref[pl.ds(j, rows, stride=SSM_SUB), :]
        lhs_scr[:, LANES * j:LANES * (j + 1)] = piece.astype(BF16)
    lhs = lhs_scr[...]
    v_scr[...] = jnp.dot(lhs, w_ref[0], preferred_element_type=F32)
    a_re = are_ref[0]
    a_im = aim_ref[0]

    def body(r, carry):
        s_re, s_im = carry
        idx = i * rows + r
        seq = idx // cps
        start = (idx % cps) == 0
        s_re = jnp.where(start, h0re_ref[pl.ds(seq, 1), :], s_re)
        s_im = jnp.where(start, h0im_ref[pl.ds(seq, 1), :], s_im)
        sin_scr[pl.ds(r, 1), 0:half] = s_re
        sin_scr[pl.ds(r, 1), half:2 * half] = s_im
        v_re = v_scr[pl.ds(r, 1), 0:half]
        v_im = v_scr[pl.ds(r, 1), half:2 * half]
        n_re = a_re * s_re - a_im * s_im + v_re
        n_im = a_re * s_im + a_im * s_re + v_im
        hre_ref[pl.ds(seq, 1), :] = n_re
        him_ref[pl.ds(seq, 1), :] = n_im
        return n_re, n_im

    s_re, s_im = lax.fori_loop(0, rows, body, (cre_scr[...], cim_scr[...]))
    cre_scr[...] = s_re
    cim_scr[...] = s_im

    y = (jnp.dot(lhs, m_ref[0], preferred_element_type=F32)
         + jnp.dot(sin_scr[...].astype(BF16), z_ref[0], preferred_element_type=F32))
    d = d_ref[...]
    for j in range(SSM_SUB):
        uj = u_ref[pl.ds(j, rows, stride=SSM_SUB), :]
        y_ref[pl.ds(j, rows, stride=SSM_SUB), :] = y[:, LANES * j:LANES * (j + 1)] + d * uj


def _ssm(u, tables, d_row, h0_re, h0_im, seq_len, tm):
    m, w, z, a_re, a_im = tables
    t, width = u.shape
    n_oct = width // LANES
    nseq = h0_re.shape[0]
    rows = tm // SSM_SUB
    half = a_re.shape[-1]
    kdim = SSM_SUB * LANES
    once = pl.Buffered(1)
    col = lambda g, i: (i, g)
    per_oct3 = lambda g, i: (g, 0, 0)
    per_oct2 = lambda g, i: (0, g)
    return pl.pallas_call(
        functools.partial(_ssm_kernel, rows=rows, cps=seq_len // SSM_SUB),
        grid=(n_oct, t // tm),
        in_specs=[
            pl.BlockSpec((tm, LANES), col),
            pl.BlockSpec((1, kdim, kdim), per_oct3, pipeline_mode=once),
            pl.BlockSpec((1, kdim, 2 * half), per_oct3, pipeline_mode=once),
            pl.BlockSpec((1, 2 * half, kdim), per_oct3, pipeline_mode=once),
            pl.BlockSpec((1, 1, half), per_oct3),
            pl.BlockSpec((1, 1, half), per_oct3),
            pl.BlockSpec((1, LANES), per_oct2),
            pl.BlockSpec((nseq, half), per_oct2),
            pl.BlockSpec((nseq, half), per_oct2),
        ],
        out_specs=[
            pl.BlockSpec((tm, LANES), col),
            pl.BlockSpec((nseq, half), per_oct2),
            pl.BlockSpec((nseq, half), per_oct2),
        ],
        out_shape=[
            jax.ShapeDtypeStruct((t, width), F32),
            jax.ShapeDtypeStruct((nseq, n_oct * half), F32),
            jax.ShapeDtypeStruct((nseq, n_oct * half), F32),
        ],
        scratch_shapes=[
            pltpu.VMEM((rows, kdim), BF16),
            pltpu.VMEM((rows, 2 * half), F32),
            pltpu.VMEM((rows, 2 * half), F32),
            pltpu.VMEM((1, half), F32),
            pltpu.VMEM((1, half), F32),
        ],
        compiler_params=_cparams(("arbitrary", "arbitrary")),
    )(u, m, w, z, a_re, a_im, d_row, h0_re, h0_im)


def _rel_bucket(rel):
    nb = REL_BUCKETS // 2
    max_exact = nb // 2
    n = jnp.abs(rel)
    nf = jnp.maximum(n, 1).astype(F32)
    large = max_exact + (jnp.log(nf / max_exact) / math.log(REL_MAX_DIST / max_exact)
                         * (nb - max_exact)).astype(jnp.int32)
    large = jnp.minimum(large, nb - 1)
    return jnp.where(rel > 0, nb, 0) + jnp.where(n < max_exact, n, large)


def _bias_table(rel_bias, qpos, kpos):
    bias = jnp.transpose(rel_bias.astype(F32)[_rel_bucket(kpos[None, :] - qpos[:, None])], (2, 0, 1))
    mask = (kpos[None, :] // CHUNK) <= (qpos[:, None] // CHUNK)
    return jnp.where(mask[None], bias, NEG_INF)


def _split_heads_lhs(q):
    lane = lax.broadcasted_iota(jnp.int32, q.shape, 1)
    zero = jnp.zeros_like(q)
    return jnp.concatenate([jnp.where(lane < ATTN_HEAD_DIM, q, zero),
                            jnp.where(lane >= ATTN_HEAD_DIM, q, zero)], axis=0)


def _nt_dot(a, b):
    return lax.dot_general(a, b, (((1,), (1,)), ((), ())), preferred_element_type=F32)


def _diff_finish(acc, l, lam, g, tq, post_scale):
    o = acc[:tq] / l[:tq] - lam * (acc[tq:] / l[tq:])
    ms = jnp.mean(o * o, axis=-1, keepdims=True)
    return o * lax.rsqrt(ms + RMS_EPS) * g * post_scale


def _flash_kernel(lam_ref, qt_ref, k_ref, vt_ref, bias_ref, g_ref, o_ref,
                  lhs_scr, s0_scr, s1_scr, m_scr, acc_scr, *, tq, tk, cb, post_scale):
    qi = pl.program_id(2)
    qt = qt_ref[0]
    row = lax.broadcasted_iota(jnp.int32, qt.shape, 0)
    zero = jnp.zeros_like(qt)
    lhs_scr[:, 0:tq] = jnp.where(row < ATTN_HEAD_DIM, qt, zero)
    lhs_scr[:, tq:2 * tq] = jnp.where(row >= ATTN_HEAD_DIM, qt, zero)
    m_scr[...] = jnp.full_like(m_scr, NEG_INF)
    acc_scr[...] = jnp.zeros_like(acc_scr)
    ones_rows = jnp.ones((acc_scr.shape[0] - LANES, tk), BF16)
    s_bufs = (s0_scr, s1_scr)

    def qk(tile, buf):
        k = k_ref[pl.ds(pl.multiple_of(tile * tk, tk), tk), :]
        s_bufs[buf][...] = jnp.dot(k, lhs_scr[...], preferred_element_type=F32)

    def softmax_pv(tile, buf, bias):
        vt = vt_ref[0, :, pl.ds(pl.multiple_of(tile * tk, tk), tk)]
        vt = jnp.concatenate([vt, ones_rows], axis=0)
        for c in range(2 * tq // cb):
            cols = pl.ds(c * cb, cb)
            s = s_bufs[buf][:, cols]
            if bias is not None:
                b0 = (c * cb) % tq
                s = s + bias[:, b0:b0 + cb]
            m_prev = m_scr[:, cols]
            m_new = jnp.maximum(m_prev, jnp.max(s, axis=0, keepdims=True))
            alpha = jnp.exp2(m_prev - m_new)
            p = jnp.exp2(s - m_new).astype(BF16)
            acc_scr[:, cols] = alpha * acc_scr[:, cols] + jnp.dot(vt, p, preferred_element_type=F32)
            m_scr[:, cols] = m_new

    qk(0, 0)

    def far_pair(p, carry):
        qk(2 * p + 1, 1)
        softmax_pv(2 * p, 0, None)
        qk(2 * p + 2, 0)
        softmax_pv(2 * p + 1, 1, None)
        return carry

    lax.fori_loop(0, jnp.maximum(qi - 1, 0), far_pair, 0)

    @pl.when(qi > 0)
    def _():
        qk(2 * qi - 1, 1)
        softmax_pv(2 * qi - 2, 0, None)
        qk(2 * qi, 0)
        softmax_pv(2 * qi - 1, 1, bias_ref[0, 0])

    qk(2 * qi + 1, 1)
    softmax_pv(2 * qi, 0, bias_ref[0, 1])
    softmax_pv(2 * qi + 1, 1, bias_ref[0, 2])
    acc = acc_scr[...]
    l = acc[LANES:LANES + 1]
    o = acc[:LANES, :tq] / l[:, :tq] - lam_ref[0] * (acc[:LANES, tq:] / l[:, tq:])
    ms = jnp.mean(o * o, axis=0, keepdims=True)
    o = o * lax.rsqrt(ms + RMS_EPS) * (g_ref[...] * post_scale)
    o_ref[...] = o.T.astype(o_ref.dtype)


def _attn_prompt(qt, kb, vt, rel_bias, lam, g, batch, seq_len, post_scale, tq):
    tk = tq // 2
    assert tk >= REL_MAX_DIST and tk % CHUNK == 0 and seq_len % tq == 0
    nh, _, t = qt.shape
    nq = seq_len // tq
    q_pos = tk + jnp.arange(tq, dtype=jnp.int32)
    k_pos = jnp.arange(3 * tk, dtype=jnp.int32)
    table = _bias_table(rel_bias, q_pos, k_pos)
    far_bias = rel_bias.astype(F32)[_rel_bucket(jnp.int32(-(tk + 1)))]
    table = jnp.swapaxes(table - far_bias[:, None, None], 1, 2) * LOG2E
    table = table.reshape(nh, 3, tk, tq)
    g_col = jnp.broadcast_to(g.reshape(LANES, 1), (LANES, tq))
    acc_rows = LANES + 16
    return pl.pallas_call(
        functools.partial(_flash_kernel, tq=tq, tk=tk, cb=ATTN_COLS, post_scale=post_scale),
        grid=(batch, nh, nq),
        in_specs=[
            pl.BlockSpec(memory_space=pltpu.SMEM),
            pl.BlockSpec((1, LANES, tq), lambda b, h, qi: (h, 0, b * nq + qi)),
            pl.BlockSpec((seq_len, LANES), lambda b, h, qi: (b, h)),
            pl.BlockSpec((1, LANES, seq_len), lambda b, h, qi: (h, 0, b)),
            pl.BlockSpec((1, 3, tk, tq), lambda b, h, qi: (h, 0, 0, 0)),
            pl.BlockSpec((LANES, tq), lambda b, h, qi: (0, 0)),
        ],
        out_specs=pl.BlockSpec((tq, LANES), lambda b, h, qi: (b * nq + qi, h)),
        out_shape=jax.ShapeDtypeStruct((t, nh * LANES), BF16),
        scratch_shapes=[
            pltpu.VMEM((LANES, 2 * tq), BF16),
            pltpu.VMEM((tk, 2 * tq), F32),
            pltpu.VMEM((tk, 2 * tq), F32),
            pltpu.VMEM((1, 2 * tq), F32),
            pltpu.VMEM((acc_rows, 2 * tq), F32),
        ],
        compiler_params=_cparams(("arbitrary", "arbitrary", "arbitrary")),
    )(lam, qt, kb, vt, table, g_col)


def _sample_attn_kernel(lam_ref, q_ref, ck_ref, cv_ref, kn_ref, vn_ref, bias_ref, g_ref, o_ref,
                        *, past, post_scale):
    tq = q_ref.shape[0]
    lhs = _split_heads_lhs(q_ref[...])
    bias = bias_ref[0]
    bias2 = jnp.concatenate([bias, bias], axis=0)
    s_c = _nt_dot(lhs, ck_ref[0].astype(BF16)) + bias2[:, :past]
    s_n = _nt_dot(lhs, kn_ref[...]) + bias2[:, past:]
    m = jnp.maximum(jnp.max(s_c, axis=1, keepdims=True), jnp.max(s_n, axis=1, keepdims=True))
    p_c = jnp.exp(s_c - m)
    p_n = jnp.exp(s_n - m)
    l = jnp.sum(p_c, axis=1, keepdims=True) + jnp.sum(p_n, axis=1, keepdims=True)
    acc = (jnp.dot(p_c.astype(BF16), cv_ref[0].astype(BF16), preferred_element_type=F32)
           + jnp.dot(p_n.astype(BF16), vn_ref[...], preferred_element_type=F32))
    o = _diff_finish(acc, l, lam_ref[0], g_ref[...], tq, post_scale)
    o_ref[...] = o.astype(o_ref.dtype)


def _attn_sample(qb, kb, vb, cache_k, cache_v, rel_bias, lam, g_row, post_scale):
    nstream, past, width = cache_k.shape
    nh = width // LANES
    ls = qb.shape[0] // nstream
    q_pos = past + jnp.arange(ls, dtype=jnp.int32)
    k_pos = jnp.arange(past + ls, dtype=jnp.int32)
    table = _bias_table(rel_bias, q_pos, k_pos)
    new = lambda b, h: (b, h)
    old = lambda b, h: (b, 0, h)
    return pl.pallas_call(
        functools.partial(_sample_attn_kernel, past=past, post_scale=post_scale),
        grid=(nstream, nh),
        in_specs=[
            pl.BlockSpec(memory_space=pltpu.SMEM),
            pl.BlockSpec((ls, LANES), new),
            pl.BlockSpec((1, past, LANES), old),
            pl.BlockSpec((1, past, LANES), old),
            pl.BlockSpec((ls, LANES), new),
            pl.BlockSpec((ls, LANES), new),
            pl.BlockSpec((1, ls, past + ls), lambda b, h: (h, 0, 0)),
            pl.BlockSpec((1, LANES), lambda b, h: (0, 0)),
        ],
        out_specs=pl.BlockSpec((ls, LANES), new),
        out_shape=jax.ShapeDtypeStruct(qb.shape, BF16),
        compiler_params=_cparams(("arbitrary", "arbitrary")),
    )(lam, qb, cache_k, cache_v, kb, vb, table, g_row)


def _mix_kernel(x_ref, y_ref, a_ref, wglu_ref, bglu_ref, wout_ref, g1_ref, b1_ref,
                wrh_ref, wrl_ref, br_ref, h_ref, lg_ref):
    half = y_ref.shape[1]
    g = jax.nn.gelu(y_ref[...])
    t = jnp.dot(g.astype(BF16), wglu_ref[...], preferred_element_type=F32) + bglu_ref[...]
    so = g * jax.nn.sigmoid(t)
    mix = (jnp.dot(so.astype(BF16), wout_ref[0:half, :], preferred_element_type=F32)
           + jnp.dot(a_ref[...], wout_ref[half:2 * half, :], preferred_element_type=F32))
    h = _layer_norm(DN_ALPHA * x_ref[...] + mix, g1_ref[...], b1_ref[...])
    h_ref[...] = h
    h_hi = h.astype(BF16)
    h_lo = (h - h_hi.astype(F32)).astype(BF16)
    lg = (jnp.dot(h_hi, wrh_ref[...], preferred_element_type=F32)
          + jnp.dot(h_lo, wrh_ref[...], preferred_element_type=F32)
          + jnp.dot(h_hi, wrl_ref[...], preferred_element_type=F32))
    lg_ref[...] = lg + br_ref[...]


def _mix(x2d, y_ssm, attn, wglu_b, bglu, wout_b, g1, b1, wr_hi, wr_lo, br, tm):
    t, d = x2d.shape
    half = d // 2
    row = lambda i: (i, 0)
    fix = lambda i: (0, 0)
    once = pl.Buffered(1)
    return pl.pallas_call(
        _mix_kernel,
        grid=(t // tm,),
        in_specs=[
            pl.BlockSpec((tm, d), row),
            pl.BlockSpec((tm, half), row),
            pl.BlockSpec((tm, half), row),
            pl.BlockSpec((half, half), fix, pipeline_mode=once),
            pl.BlockSpec((1, half), fix),
            pl.BlockSpec((d, d), fix, pipeline_mode=once),
            pl.BlockSpec((1, d), fix),
            pl.BlockSpec((1, d), fix),
            pl.BlockSpec((d, LANES), fix),
            pl.BlockSpec((d, LANES), fix),
            pl.BlockSpec((1, LANES), fix),
        ],
        out_specs=[pl.BlockSpec((tm, d), row), pl.BlockSpec((tm, LANES), row)],
        out_shape=[jax.ShapeDtypeStruct((t, d), F32), jax.ShapeDtypeStruct((t, LANES), F32)],
        compiler_params=_cparams(("arbitrary",)),
    )(x2d, y_ssm, attn, wglu_b, bglu, wout_b, g1, b1, wr_hi, wr_lo, br)


def _route_kernel(lg_ref, eid_ref, gate_ref, rank_ref, cnt_ref):
    i = pl.program_id(0)
    tm = lg_ref.shape[0]
    ng = N_EXPERT_GROUPS

    @pl.when(i == 0)
    def _():
        cnt_ref[...] = jnp.zeros_like(cnt_ref)

    lg = lg_ref[...]
    lane = lax.broadcasted_iota(jnp.int32, lg.shape, 1)
    is_g = lane < ng
    l1 = jnp.where(is_g, lg, -jnp.inf)
    m1 = jnp.max(l1, axis=1, keepdims=True)
    z1 = jnp.sum(jnp.where(is_g, jnp.exp(l1 - m1), 0.0), axis=1, keepdims=True)
    w_grp = 1.0 / z1
    el = (lane - ng).astype(F32)
    none = float(LANES)
    grp = jnp.min(jnp.where(l1 == m1, lane.astype(F32), none), axis=1, keepdims=True)
    lo = grp * EXPERTS_PER_GROUP
    in_grp = jnp.where(el >= lo, jnp.where(el < lo + EXPERTS_PER_GROUP, 1.0, 0.0), 0.0) > 0.5
    l2 = jnp.where(in_grp, lg, -jnp.inf)
    t1 = jnp.max(l2, axis=1, keepdims=True)
    i1 = jnp.min(jnp.where(l2 == t1, el, none), axis=1, keepdims=True)
    l2b = jnp.where(el == i1, -jnp.inf, l2)
    t2 = jnp.max(l2b, axis=1, keepdims=True)
    i2 = jnp.min(jnp.where(l2b == t2, el, none), axis=1, keepdims=True)
    e2 = jnp.exp(t2 - t1)
    den = 1.0 + e2
    g0 = w_grp * (1.0 / den)
    g1 = w_grp * (e2 / den)
    hit1 = el == i1
    hit2 = el == i2
    oh = jnp.where(hit1, 1.0, jnp.where(hit2, 1.0, 0.0))
    r_id = lax.broadcasted_iota(jnp.int32, (tm, tm), 0)
    c_id = lax.broadcasted_iota(jnp.int32, (tm, tm), 1)
    tri = jnp.where(c_id < r_id, 1.0, 0.0).astype(BF16)
    before = jnp.dot(tri, oh.astype(BF16), preferred_element_type=F32) + cnt_ref[...]
    rank1 = jnp.sum(jnp.where(hit1, before, 0.0), axis=1, keepdims=True).astype(jnp.int32)
    rank2 = jnp.sum(jnp.where(hit2, before, 0.0), axis=1, keepdims=True).astype(jnp.int32)
    cnt_ref[...] = cnt_ref[...] + jnp.sum(oh, axis=0, keepdims=True)
    eid_ref[...] = jnp.where(lane == 0, i1, jnp.where(lane == 1, i2, 0.0)).astype(jnp.int32)
    gate_ref[...] = jnp.where(lane == 0, g0, jnp.where(lane == 1, g1, 0.0))
    rank_ref[...] = jnp.where(lane == 0, rank1, jnp.where(lane == 1, rank2, 0))


def _route(logits, tm):
    t = logits.shape[0]
    row = lambda i: (i, 0)
    return pl.pallas_call(
        _route_kernel,
        grid=(t // tm,),
        in_specs=[pl.BlockSpec((tm, LANES), row)],
        out_specs=[pl.BlockSpec((tm, LANES), row)] * 3 + [pl.BlockSpec((1, LANES), lambda i: (0, 0))],
        out_shape=[
            jax.ShapeDtypeStruct((t, LANES), jnp.int32),
            jax.ShapeDtypeStruct((t, LANES), F32),
            jax.ShapeDtypeStruct((t, LANES), jnp.int32),
            jax.ShapeDtypeStruct((1, LANES), F32),
        ],
        compiler_params=_cparams(("arbitrary",)),
    )(logits)


def _row_copy(src, s_row, dst, d_row, sem):
    return pltpu.make_async_copy(src.at[pl.ds(s_row, 1), :], dst.at[pl.ds(d_row, 1), :], sem)


def _dispatch_kernel(dest_ref, h_ref, xin_ref, xbuf_ref, sem):
    del xin_ref
    tm = h_ref.shape[0]

    def issue(r, c):
        _row_copy(h_ref, r, xbuf_ref, dest_ref[0, 0, 2 * r], sem).start()
        _row_copy(h_ref, r, xbuf_ref, dest_ref[0, 0, 2 * r + 1], sem).start()
        return c

    lax.fori_loop(0, tm, issue, 0)

    def drain(r, c):
        _row_copy(h_ref, 0, xbuf_ref, 0, sem).wait()
        _row_copy(h_ref, 0, xbuf_ref, 0, sem).wait()
        return c

    lax.fori_loop(0, tm, drain, 0)


def _dispatch(h, dest3, n_rows, tm):
    t, d = h.shape
    xbuf0 = jnp.zeros((n_rows, d), h.dtype)
    return pl.pallas_call(
        _dispatch_kernel,
        grid=(t // tm,),
        in_specs=[
            pl.BlockSpec((1, 1, 2 * tm), lambda i: (i, 0, 0), memory_space=pltpu.SMEM),
            pl.BlockSpec((tm, d), lambda i: (i, 0)),
            pl.BlockSpec(memory_space=pl.ANY),
        ],
        out_specs=pl.BlockSpec(memory_space=pl.ANY),
        out_shape=jax.ShapeDtypeStruct((n_rows, d), h.dtype),
        scratch_shapes=[pltpu.SemaphoreType.DMA(())],
        input_output_aliases={2: 0},
        compiler_params=_cparams(("arbitrary",)),
    )(dest3, h, xbuf0)


def _expert_kernel(be_ref, nused_ref, x_ref, wg_ref, wu_ref, wd_ref, y_ref):
    i = pl.program_id(0)

    @pl.when(i < nused_ref[0])
    def _():
        x = x_ref[...].astype(BF16)
        gt = jnp.dot(x, wg_ref[0], preferred_element_type=F32)
        up = jnp.dot(x, wu_ref[0], preferred_element_type=F32)
        a = (jax.nn.silu(gt) * up).astype(BF16)
        y_ref[...] = jnp.dot(a, wd_ref[0], preferred_element_type=F32)

    @pl.when(i >= nused_ref[0])
    def _():
        y_ref[...] = jnp.zeros_like(y_ref)


def _experts(xbuf, block_e, n_used, wg_b, wu_b, wd_b):
    n_rows, d = xbuf.shape
    de = wg_b.shape[-1]
    nb = n_rows // MOE_ROWS
    wmap = lambda i, be, nu: (be[i], 0, 0)
    grid_spec = pltpu.PrefetchScalarGridSpec(
        num_scalar_prefetch=2,
        grid=(nb,),
        in_specs=[
            pl.BlockSpec((MOE_ROWS, d), lambda i, be, nu: (i, 0)),
            pl.BlockSpec((1, d, de), wmap),
            pl.BlockSpec((1, d, de), wmap),
            pl.BlockSpec((1, de, d), wmap),
        ],
        out_specs=pl.BlockSpec((MOE_ROWS, d), lambda i, be, nu: (i, 0)),
    )
    return pl.pallas_call(
        _expert_kernel,
        grid_spec=grid_spec,
        out_shape=jax.ShapeDtypeStruct((n_rows, d), F32),
        compiler_params=_cparams(("arbitrary",)),
    )(block_e, n_used, xbuf, wg_b, wu_b, wd_b)


def _combine_kernel(dest_ref, h_ref, gate_ref, g2_ref, b2_ref, ybuf_ref, o_ref, ya_scr, yb_scr, sem):
    tm = h_ref.shape[0]

    def issue(r, c):
        _row_copy(ybuf_ref, dest_ref[0, 0, 2 * r], ya_scr, r, sem).start()
        _row_copy(ybuf_ref, dest_ref[0, 0, 2 * r + 1], yb_scr, r, sem).start()
        return c

    lax.fori_loop(0, tm, issue, 0)

    def drain(r, c):
        _row_copy(ybuf_ref, 0, ya_scr, 0, sem).wait()
        _row_copy(ybuf_ref, 0, yb_scr, 0, sem).wait()
        return c

    lax.fori_loop(0, tm, drain, 0)
    gate = gate_ref[...]
    f = ya_scr[...] * gate[:, 0:1] + yb_scr[...] * gate[:, 1:2]
    o_ref[...] = _layer_norm(DN_ALPHA * h_ref[...] + f, g2_ref[...], b2_ref[...])


def _combine(h, gate, dest3, ybuf, g2, b2, tm):
    t, d = h.shape
    row = lambda i: (i, 0)
    fix = lambda i: (0, 0)
    return pl.pallas_call(
        _combine_kernel,
        grid=(t // tm,),
        in_specs=[
            pl.BlockSpec((1, 1, 2 * tm), lambda i: (i, 0, 0), memory_space=pltpu.SMEM),
            pl.BlockSpec((tm, d), row),
            pl.BlockSpec((tm, LANES), row),
            pl.BlockSpec((1, d), fix),
            pl.BlockSpec((1, d), fix),
            pl.BlockSpec(memory_space=pl.ANY),
        ],
        out_specs=pl.BlockSpec((tm, d), row),
        out_shape=jax.ShapeDtypeStruct((t, d), F32),
        scratch_shapes=[pltpu.VMEM((tm, d), F32), pltpu.VMEM((tm, d), F32), pltpu.SemaphoreType.DMA(())],
        compiler_params=_cparams(("arbitrary",)),
    )(dest3, h, gate, g2, b2, ybuf)


def _moe_finish(h, logits, wg_b, wu_b, wd_b, g2, b2, tm_route, tm_rows):
    t = h.shape[0]
    eid, gate, rank, cnt = _route(logits, tm_route)
    counts = cnt[0, N_EXPERT_GROUPS:N_EXPERT_GROUPS + N_EXPERTS].astype(jnp.int32)
    padded = (counts + MOE_ROWS - 1) // MOE_ROWS * MOE_ROWS
    pad_end = jnp.cumsum(padded)
    pad_start = pad_end - padded
    experts = jnp.arange(N_EXPERTS, dtype=jnp.int32)
    start_of = jnp.sum(jnp.where(eid[:, :2, None] == experts, pad_start, 0), axis=-1)
    dest = start_of + rank[:, :2]
    n_blocks = -(-(2 * t) // MOE_ROWS) + N_EXPERTS
    first_row = jnp.arange(n_blocks, dtype=jnp.int32)[:, None] * MOE_ROWS
    block_e = jnp.minimum(jnp.sum((pad_end[None, :] <= first_row).astype(jnp.int32), axis=1),
                          N_EXPERTS - 1)
    n_used = (pad_end[-1:] // MOE_ROWS).astype(jnp.int32)
    dest3 = dest.astype(jnp.int32).reshape(t // tm_rows, 1, 2 * tm_rows)
    xbuf = _dispatch(h, dest3, n_blocks * MOE_ROWS, tm_rows)
    ybuf = _experts(xbuf, block_e, n_used, wg_b, wu_b, wd_b)
    return _combine(h, gate, dest3, ybuf, g2, b2, tm_rows)


def _pick(t, pref):
    tm = min(pref, t)
    while t % tm:
        tm //= 2
    return tm


def _stream(x, h0_re, h0_im, cache_k, cache_v, params, lam, lam_init):
    (w_in_b, tables, d_row, wglu_b, bglu, rel_bias, g_row, wout_b, g1, b1,
     wr_hi, wr_lo, br, wg_b, wu_b, wd_b, g2, b2) = params
    b, l, d = x.shape
    t = b * l
    x2d = x.reshape(t, d)
    u, k, v, qb, kb, vb, qt, vt = _in_proj(x2d, w_in_b, _pick(t, 512))
    y_ssm, h_re, h_im = _ssm(u, tables, d_row, h0_re, h0_im, l, _pick(t, 4096))
    post = 1.0 - lam_init
    if cache_k is None:
        attn = _attn_prompt(qt, kb, vt, rel_bias, lam, g_row, b, l, post, _pick(l, ATTN_TILE))
    else:
        attn = _attn_sample(qb, kb, vb, cache_k, cache_v, rel_bias, lam, g_row, post)
    h, logits = _mix(x2d, y_ssm, attn, wglu_b, bglu, wout_b, g1, b1, wr_hi, wr_lo, br, _pick(t, 256))
    out = _moe_finish(h, logits, wg_b, wu_b, wd_b, g2, b2, _pick(t, 512), _pick(t, 256))
    return out.reshape(b, l, d), k, v, h_re, h_im


def kernel(x_prompt, x_sample, cache_k, cache_v, state_ssm_re, state_ssm_im, w_in, ssm_a_re, ssm_a_im, ssm_log_dt, ssm_b_re, ssm_b_im, ssm_c_re, ssm_c_im, ssm_d, w_glu, b_glu, lambda_q1, lambda_k1, lambda_q2, lambda_k2, subln_g, rel_bias, w_out, ln1_g, ln1_b, w_r1, b_r1, w_r2, b_r2, w_gate, w_up, w_down, ln2_g, ln2_b):
    assert w_in.shape[0] == DEPTH
    bp, lp, d = x_prompt.shape
    bs, ls, _ = x_sample.shape
    past = cache_k.shape[2]
    nh, dqk = cache_k.shape[3], cache_k.shape[4]
    n_grp, n_state = state_ssm_re.shape[2], state_ssm_re.shape[3]
    l = 0
    lam_init = 0.8 - 0.6 * math.exp(-0.3 * l)
    lam = (jnp.exp(jnp.sum(lambda_q1[l].astype(F32) * lambda_k1[l].astype(F32)))
           - jnp.exp(jnp.sum(lambda_q2[l].astype(F32) * lambda_k2[l].astype(F32))) + lam_init).reshape(1)
    tables = _ssm_tables(ssm_a_re[l].astype(F32), ssm_a_im[l].astype(F32), ssm_log_dt[l].astype(F32),
                         ssm_b_re[l].astype(F32), ssm_b_im[l].astype(F32),
                         ssm_c_re[l].astype(F32), ssm_c_im[l].astype(F32))
    wr = jnp.concatenate([w_r1[l].astype(F32), w_r2[l].astype(F32).reshape(d, N_EXPERTS)], axis=1)
    wr = jnp.pad(wr, ((0, 0), (0, LANES - wr.shape[1])))
    wr_hi = wr.astype(BF16)
    wr_lo = (wr - wr_hi.astype(F32)).astype(BF16)
    br = jnp.concatenate([b_r1[l].astype(F32), b_r2[l].astype(F32).reshape(N_EXPERTS)])
    br = jnp.pad(br, (0, LANES - br.shape[0])).reshape(1, LANES)
    params = (
        w_in[l].astype(BF16), tables, ssm_d[l].astype(F32).reshape(1, -1),
        w_glu[l].astype(BF16), b_glu[l].astype(F32).reshape(1, -1), rel_bias,
        subln_g[l].astype(F32).reshape(1, -1), w_out[l].astype(BF16),
        ln1_g[l].astype(F32).reshape(1, -1), ln1_b[l].astype(F32).reshape(1, -1),
        wr_hi, wr_lo, br,
        w_gate[l].astype(BF16), w_up[l].astype(BF16), w_down[l].astype(BF16),
        ln2_g[l].astype(F32).reshape(1, -1), ln2_b[l].astype(F32).reshape(1, -1),
    )
    zeros = jnp.zeros((bp, n_grp * n_state), F32)
    yp, kp, vp, hp_re, hp_im = _stream(x_prompt, zeros, zeros, None, None, params, lam, lam_init)
    ys, ks, vs, hs_re, hs_im = _stream(
        x_sample, state_ssm_re[l].astype(F32).reshape(bs, -1), state_ssm_im[l].astype(F32).reshape(bs, -1),
        cache_k[l].reshape(bs, past, nh * dqk), cache_v[l].reshape(bs, past, -1), params, lam, lam_init)
    return (yp, ys,
            kp.reshape(1, bp, lp, nh, dqk).astype(cache_k.dtype),
            vp.reshape(1, bp, lp, nh, -1).astype(cache_v.dtype),
            hp_re.reshape(1, bp, n_grp, n_state).astype(state_ssm_re.dtype),
            hp_im.reshape(1, bp, n_grp, n_state).astype(state_ssm_im.dtype),
            ks.reshape(1, bs, ls, nh, dqk).astype(cache_k.dtype),
            vs.reshape(1, bs, ls, nh, -1).astype(cache_v.dtype),
            hs_re.reshape(1, bs, n_grp, n_state).astype(state_ssm_re.dtype),
            hs_im.reshape(1, bs, n_grp, n_state).astype(state_ssm_im.dtype))
```

```python
import functools
import math

import jax
import jax.numpy as jnp
from jax import lax
from jax.experimental import pallas as pl
from jax.experimental.pallas import tpu as pltpu

F32 = jnp.float32
BF16 = jnp.bfloat16

CHUNK = 64
SSM_GROUP_CH = 16
SSM_STATE = 64
N_ATTN_HEADS = 8
ATTN_HEAD_DIM = 64
REL_BUCKETS = 32
REL_MAX_DIST = 128
N_EXPERT_GROUPS = 4
EXPERTS_PER_GROUP = 8
N_EXPERTS = N_EXPERT_GROUPS * EXPERTS_PER_GROUP
DEPTH = 1
DN_ALPHA = (2 * DEPTH) ** 0.25
LN_EPS = 1e-5
RMS_EPS = 1e-5
NEG_INF = -1e30
LOG2E = math.log2(math.e)

LANES = 128
VMEM_LIMIT = 56 * 1024 * 1024

SSM_SUB = 16
SSM_OCT = LANES // SSM_GROUP_CH
ATTN_TILE = 1024
ATTN_COLS = 256
MOE_ROWS = 256
DMA_UNROLL = 8


def _cparams(sem, vmem=VMEM_LIMIT):
    return pltpu.CompilerParams(dimension_semantics=sem, vmem_limit_bytes=vmem)


def _layer_norm(r, g, b):
    mu = jnp.mean(r, axis=-1, keepdims=True)
    rc = r - mu
    var = jnp.mean(rc * rc, axis=-1, keepdims=True)
    return rc * lax.rsqrt(var + LN_EPS) * g + b


def _store_heads_transposed(dst_ref, z):
    for h in range(dst_ref.shape[0]):
        dst_ref[h] = z[:, LANES * h:LANES * (h + 1)].T.astype(BF16)


def _store_heads_split(dst_ref, z):
    for h in range(dst_ref.shape[1]):
        dst_ref[:, h, :] = z[:, LANES * h:LANES * (h + 1)]


def _in_proj_kernel(x_ref, w_ref, u_ref, k_ref, v_ref, qb_ref, kb_ref, vb_ref, qt_ref, vt_ref, xb_scr,
                    *, q_scale):
    j = pl.program_id(1)

    @pl.when(j == 0)
    def _():
        xb_scr[...] = x_ref[...].astype(BF16)

    z = jnp.dot(xb_scr[...], w_ref[...], preferred_element_type=F32)

    @pl.when(j == 0)
    def _():
        u_ref[...] = z

    @pl.when(j == 1)
    def _():
        zq = z * q_scale
        qb_ref[...] = zq.astype(BF16)
        _store_heads_transposed(qt_ref, zq * LOG2E)

    @pl.when(j == 2)
    def _():
        _store_heads_split(k_ref, z)
        kb_ref[...] = z.astype(BF16)

    @pl.when(j == 3)
    def _():
        _store_heads_split(v_ref, z)
        vb_ref[...] = z.astype(BF16)
        _store_heads_transposed(vt_ref, z)


def _in_proj(x2d, w_in_b, tm):
    t, d = x2d.shape
    n = w_in_b.shape[1] // 4
    nh = n // LANES
    row = lambda i, j: (i, 0)
    f32_out = jax.ShapeDtypeStruct((t, n), F32)
    b16_out = jax.ShapeDtypeStruct((t, n), BF16)
    tr_out = jax.ShapeDtypeStruct((nh, LANES, t), BF16)
    tr_spec = pl.BlockSpec((nh, LANES, tm), lambda i, j: (0, 0, i))
    cache_out = jax.ShapeDtypeStruct((t, nh, LANES), F32)
    cache_spec = pl.BlockSpec((tm, nh, LANES), lambda i, j: (i, 0, 0))
    return pl.pallas_call(
        functools.partial(_in_proj_kernel, q_scale=ATTN_HEAD_DIM ** -0.5),
        grid=(t // tm, 4),
        in_specs=[pl.BlockSpec((tm, d), row), pl.BlockSpec((d, n), lambda i, j: (0, j))],
        out_specs=[pl.BlockSpec((tm, n), row), cache_spec, cache_spec] + [pl.BlockSpec((tm, n), row)] * 3
        + [tr_spec] * 2,
        out_shape=[f32_out, cache_out, cache_out, b16_out, b16_out, b16_out, tr_out, tr_out],
        scratch_shapes=[pltpu.VMEM((tm, d), BF16)],
        compiler_params=_cparams(("arbitrary", "arbitrary")),
    )(x2d, w_in_b)


def _ssm_tables(a_re, a_im, log_dt, b_re, b_im, c_re, c_im):
    hp = lax.Precision.HIGHEST
    g, p = a_re.shape
    c = b_re.shape[-1]
    s, o = SSM_SUB, SSM_OCT
    dt = jnp.exp(log_dt)[:, None]
    mag = jnp.exp(a_re * dt)
    ab_re = mag * jnp.cos(a_im * dt)
    ab_im = mag * jnp.sin(a_im * dt)
    nr = ab_re - 1.0
    ni = ab_im
    den = a_re * a_re + a_im * a_im
    z_re = ((nr * a_re + ni * a_im) / den)[..., None]
    z_im = ((ni * a_re - nr * a_im) / den)[..., None]
    bb_re = z_re * b_re - z_im * b_im
    bb_im = z_re * b_im + z_im * b_re
    n = jnp.arange(s + 1, dtype=F32)[:, None, None]
    pmag = jnp.exp(n * (a_re * dt))
    pw_re = pmag * jnp.cos(n * (a_im * dt))
    pw_im = pmag * jnp.sin(n * (a_im * dt))
    ca_re = c_re[None] * pw_re[:, :, None, :] - c_im[None] * pw_im[:, :, None, :]
    ca_im = c_re[None] * pw_im[:, :, None, :] + c_im[None] * pw_re[:, :, None, :]
    kern = (jnp.einsum('dgop,gpi->dgio', ca_re[:s], bb_re, precision=hp)
            - jnp.einsum('dgop,gpi->dgio', ca_im[:s], bb_im, precision=hp))
    jj = jnp.arange(s)
    lag = jj[None, :] - jj[:, None]
    kj = jnp.where((lag >= 0)[:, :, None, None, None], kern[jnp.clip(lag, 0, s - 1)], 0.0)
    kj = kj.reshape(s, s, g // o, o, c, c)
    eye = jnp.eye(o, dtype=F32)
    m = jnp.einsum('abGlic,lm->Galibmc', kj, eye)
    m = m.reshape(g // o, s * o * c, s * o * c).astype(BF16)
    nrev = (s - 1) - jnp.arange(s, dtype=F32)[:, None, None]
    rmag = jnp.exp(nrev * (a_re * dt))
    rv_re = rmag * jnp.cos(nrev * (a_im * dt))
    rv_im = rmag * jnp.sin(nrev * (a_im * dt))
    wb_re = rv_re[..., None] * bb_re[None] - rv_im[..., None] * bb_im[None]
    wb_im = rv_re[..., None] * bb_im[None] + rv_im[..., None] * bb_re[None]
    wb = jnp.stack([wb_re, wb_im], axis=0).reshape(2, s, g // o, o, p, c)
    w = jnp.einsum('rjGlpi,lm->Gjlirmp', wb, eye)
    w = w.reshape(g // o, s * o * c, 2 * o * p).astype(BF16)
    zc = jnp.stack([ca_re[1:], -ca_im[1:]], axis=0).reshape(2, s, g // o, o, c, p)
    z = jnp.einsum('rjGlop,lm->Grlpjmo', zc, eye)
    z = z.reshape(g // o, 2 * o * p, s * o * c).astype(BF16)
    a_s_re = pw_re[s].reshape(g // o, 1, o * p)
    a_s_im = pw_im[s].reshape(g // o, 1, o * p)
    return m, w, z, a_s_re, a_s_im


def _ssm_kernel(u_ref, m_ref, w_ref, z_ref, are_ref, aim_ref, d_ref, h0re_ref, h0im_ref,
                y_ref, hre_ref, him_ref, lhs_scr, v_scr, sin_scr, cre_scr, cim_scr, *, rows, cps):
    i = pl.program_id(1)
    half = cre_scr.shape[1]

    @pl.when(i == 0)
    def _():
        cre_scr[...] = jnp.zeros_like(cre_scr)
        cim_scr[...] = jnp.zeros_like(cim_scr)

    for j in range(SSM_SUB):
        piece = u_ref[pl.ds(j, rows, stride=SSM_SUB), :]
        lhs_scr[:, LANES * j:LANES * (j + 1)] = piece.astype(BF16)
    lhs = lhs_scr[...]
    v_scr[...] = jnp.dot(lhs, w_ref[0], preferred_element_type=F32)
    a_re = are_ref[0]
    a_im = aim_ref[0]

    def body(r, carry):
        s_re, s_im = carry
        idx = i * rows + r
        seq = idx // cps
        start = (idx % cps) == 0
        s_re = jnp.where(start, h0re_ref[pl.ds(seq, 1), :], s_re)
        s_im = jnp.where(start, h0im_ref[pl.ds(seq, 1), :], s_im)
        sin_scr[pl.ds(r, 1), 0:half] = s_re
        sin_scr[pl.ds(r, 1), half:2 * half] = s_im
        v_re = v_scr[pl.ds(r, 1), 0:half]
        v_im = v_scr[pl.ds(r, 1), half:2 * half]
        n_re = a_re * s_re - a_im * s_im + v_re
        n_im = a_re * s_im + a_im * s_re + v_im
        hre_ref[pl.ds(seq, 1), :] = n_re
        him_ref[pl.ds(seq, 1), :] = n_im
        return n_re, n_im

    s_re, s_im = lax.fori_loop(0, rows, body, (cre_scr[...], cim_scr[...]))
    cre_scr[...] = s_re
    cim_scr[...] = s_im

    y = (jnp.dot(lhs, m_ref[0], preferred_element_type=F32)
         + jnp.dot(sin_scr[...].astype(BF16), z_ref[0], preferred_element_type=F32))
    d = d_ref[...]
    for j in range(SSM_SUB):
        uj = u_ref[pl.ds(j, rows, stride=SSM_SUB), :]
        y_ref[pl.ds(j, rows, stride=SSM_SUB), :] = y[:, LANES * j:LANES * (j + 1)] + d * uj


def _ssm(u, tables, d_row, h0_re, h0_im, seq_len, tm):
    m, w, z, a_re, a_im = tables
    t, width = u.shape
    n_oct = width // LANES
    nseq = h0_re.shape[0]
    rows = tm // SSM_SUB
    half = a_re.shape[-1]
    kdim = SSM_SUB * LANES
    once = pl.Buffered(1)
    col = lambda g, i: (i, g)
    per_oct3 = lambda g, i: (g, 0, 0)
    per_oct2 = lambda g, i: (0, g)
    return pl.pallas_call(
        functools.partial(_ssm_kernel, rows=rows, cps=seq_len // SSM_SUB),
        grid=(n_oct, t // tm),
        in_specs=[
            pl.BlockSpec((tm, LANES), col),
            pl.BlockSpec((1, kdim, kdim), per_oct3, pipeline_mode=once),
            pl.BlockSpec((1, kdim, 2 * half), per_oct3, pipeline_mode=once),
            pl.BlockSpec((1, 2 * half, kdim), per_oct3, pipeline_mode=once),
            pl.BlockSpec((1, 1, half), per_oct3),
            pl.BlockSpec((1, 1, half), per_oct3),
            pl.BlockSpec((1, LANES), per_oct2),
            pl.BlockSpec((nseq, half), per_oct2),
            pl.BlockSpec((nseq, half), per_oct2),
        ],
        out_specs=[
            pl.BlockSpec((tm, LANES), col),
            pl.BlockSpec((nseq, half), per_oct2),
            pl.BlockSpec((nseq, half), per_oct2),
        ],
        out_shape=[
            jax.ShapeDtypeStruct((t, width), F32),
            jax.ShapeDtypeStruct((nseq, n_oct * half), F32),
            jax.ShapeDtypeStruct((nseq, n_oct * half), F32),
        ],
        scratch_shapes=[
            pltpu.VMEM((rows, kdim), BF16),
            pltpu.VMEM((rows, 2 * half), F32),
            pltpu.VMEM((rows, 2 * half), F32),
            pltpu.VMEM((1, half), F32),
            pltpu.VMEM((1, half), F32),
        ],
        compiler_params=_cparams(("arbitrary", "arbitrary")),
    )(u, m, w, z, a_re, a_im, d_row, h0_re, h0_im)


def _rel_bucket(rel):
    nb = REL_BUCKETS // 2
    max_exact = nb // 2
    n = jnp.abs(rel)
    nf = jnp.maximum(n, 1).astype(F32)
    large = max_exact + (jnp.log(nf / max_exact) / math.log(REL_MAX_DIST / max_exact)
                         * (nb - max_exact)).astype(jnp.int32)
    large = jnp.minimum(large, nb - 1)
    return jnp.where(rel > 0, nb, 0) + jnp.where(n < max_exact, n, large)


def _bias_table(rel_bias, n_k, n_q, q0, shift):
    nh = rel_bias.shape[1]
    n_rel = n_k + n_q - 1
    rel = (n_k - 1 - q0) - jnp.arange(n_rel, dtype=jnp.int32)
    by_rel = rel_bias.astype(F32)[_rel_bucket(rel)].T - shift[:, None]
    by_rel = jnp.pad(by_rel, ((0, 0), (0, 1)))
    skew = jnp.tile(by_rel, (1, n_k))[:, :n_k * n_rel].reshape(nh, n_k, n_rel)
    bias = skew[:, :, n_k - 1:n_k - 1 + n_q]
    k_chunk = lax.broadcasted_iota(jnp.int32, (n_k, n_q), 0) // CHUNK
    q_chunk = (q0 + lax.broadcasted_iota(jnp.int32, (n_k, n_q), 1)) // CHUNK
    return jnp.where((k_chunk <= q_chunk)[None], bias, NEG_INF)


def _split_heads_lhs(q):
    lane = lax.broadcasted_iota(jnp.int32, q.shape, 1)
    zero = jnp.zeros_like(q)
    return jnp.concatenate([jnp.where(lane < ATTN_HEAD_DIM, q, zero),
                            jnp.where(lane >= ATTN_HEAD_DIM, q, zero)], axis=0)


def _nt_dot(a, b):
    return lax.dot_general(a, b, (((1,), (1,)), ((), ())), preferred_element_type=F32)


def _diff_finish(acc, l, lam, g, tq, post_scale):
    o = acc[:tq] / l[:tq] - lam * (acc[tq:] / l[tq:])
    ms = jnp.mean(o * o, axis=-1, keepdims=True)
    return o * lax.rsqrt(ms + RMS_EPS) * g * post_scale


def _flash_kernel(lam_ref, qt_ref, k_ref, vt_ref, bias_ref, g_ref, o_ref,
                  lhs_scr, s0_scr, s1_scr, m_scr, acc_scr, *, tq, tk, cb, post_scale):
    qi = pl.program_id(2)
    qt = qt_ref[0]
    row = lax.broadcasted_iota(jnp.int32, qt.shape, 0)
    zero = jnp.zeros_like(qt)
    lhs_scr[:, 0:tq] = jnp.where(row < ATTN_HEAD_DIM, qt, zero)
    lhs_scr[:, tq:2 * tq] = jnp.where(row >= ATTN_HEAD_DIM, qt, zero)
    m_scr[...] = jnp.full_like(m_scr, NEG_INF)
    acc_scr[...] = jnp.zeros_like(acc_scr)
    ones_rows = jnp.ones((acc_scr.shape[0] - LANES, tk), BF16)
    s_bufs = (s0_scr, s1_scr)

    def qk(tile, buf):
        k = k_ref[pl.ds(pl.multiple_of(tile * tk, tk), tk), :]
        s_bufs[buf][...] = jnp.dot(k, lhs_scr[...], preferred_element_type=F32)

    def softmax_pv(tile, buf, bias):
        vt = vt_ref[0, :, pl.ds(pl.multiple_of(tile * tk, tk), tk)]
        vt = jnp.concatenate([vt, ones_rows], axis=0)
        for c in range(2 * tq // cb):
            cols = pl.ds(c * cb, cb)
            s = s_bufs[buf][:, cols]
            if bias is not None:
                b0 = (c * cb) % tq
                s = s + bias[:, b0:b0 + cb]
            m_prev = m_scr[:, cols]
            m_new = jnp.maximum(m_prev, jnp.max(s, axis=0, keepdims=True))
            alpha = jnp.exp2(m_prev - m_new)
            p = jnp.exp2(s - m_new).astype(BF16)
            acc_scr[:, cols] = alpha * acc_scr[:, cols] + jnp.dot(vt, p, preferred_element_type=F32)
            m_scr[:, cols] = m_new

    qk(0, 0)

    def far_pair(p, carry):
        qk(2 * p + 1, 1)
        softmax_pv(2 * p, 0, None)
        qk(2 * p + 2, 0)
        softmax_pv(2 * p + 1, 1, None)
        return carry

    lax.fori_loop(0, jnp.maximum(qi - 1, 0), far_pair, 0)

    @pl.when(qi > 0)
    def _():
        qk(2 * qi - 1, 1)
        softmax_pv(2 * qi - 2, 0, None)
        qk(2 * qi, 0)
        softmax_pv(2 * qi - 1, 1, bias_ref[0, 0])

    qk(2 * qi + 1, 1)
    softmax_pv(2 * qi, 0, bias_ref[0, 1])
    softmax_pv(2 * qi + 1, 1, bias_ref[0, 2])
    acc = acc_scr[...]
    l = acc[LANES:LANES + 1]
    o = acc[:LANES, :tq] / l[:, :tq] - lam_ref[0] * (acc[:LANES, tq:] / l[:, tq:])
    ms = jnp.mean(o * o, axis=0, keepdims=True)
    o = o * lax.rsqrt(ms + RMS_EPS) * (g_ref[...] * post_scale)
    o_ref[...] = o.T.astype(o_ref.dtype)


def _attn_prompt(qt, kb, vt, rel_bias, lam, g, batch, seq_len, post_scale, tq):
    tk = tq // 2
    assert tk >= REL_MAX_DIST and tk % CHUNK == 0 and seq_len % tq == 0
    nh, _, t = qt.shape
    nq = seq_len // tq
    far_bias = rel_bias.astype(F32)[_rel_bucket(jnp.int32(-(tk + 1)))]
    table = (_bias_table(rel_bias, 3 * tk, tq, tk, far_bias) * LOG2E).reshape(nh, 3, tk, tq)
    g_col = jnp.broadcast_to(g.reshape(LANES, 1), (LANES, tq))
    acc_rows = LANES + 16
    return pl.pallas_call(
        functools.partial(_flash_kernel, tq=tq, tk=tk, cb=ATTN_COLS, post_scale=post_scale),
        grid=(batch, nh, nq),
        in_specs=[
            pl.BlockSpec(memory_space=pltpu.SMEM),
            pl.BlockSpec((1, LANES, tq), lambda b, h, qi: (h, 0, b * nq + qi)),
            pl.BlockSpec((seq_len, LANES), lambda b, h, qi: (b, h)),
            pl.BlockSpec((1, LANES, seq_len), lambda b, h, qi: (h, 0, b)),
            pl.BlockSpec((1, 3, tk, tq), lambda b, h, qi: (h, 0, 0, 0)),
            pl.BlockSpec((LANES, tq), lambda b, h, qi: (0, 0)),
        ],
        out_specs=pl.BlockSpec((tq, LANES), lambda b, h, qi: (b * nq + qi, h)),
        out_shape=jax.ShapeDtypeStruct((t, nh * LANES), BF16),
        scratch_shapes=[
            pltpu.VMEM((LANES, 2 * tq), BF16),
            pltpu.VMEM((tk, 2 * tq), F32),
            pltpu.VMEM((tk, 2 * tq), F32),
            pltpu.VMEM((1, 2 * tq), F32),
            pltpu.VMEM((acc_rows, 2 * tq), F32),
        ],
        compiler_params=_cparams(("arbitrary", "arbitrary", "arbitrary")),
    )(lam, qt, kb, vt, table, g_col)


def _sample_attn_kernel(lam_ref, q_ref, ck_ref, cv_ref, kn_ref, vn_ref, bias_ref, g_ref, o_ref,
                        *, past, post_scale):
    tq = q_ref.shape[0]
    lhs = _split_heads_lhs(q_ref[...])
    bias = bias_ref[0]
    bias2 = jnp.concatenate([bias, bias], axis=0)
    s_c = _nt_dot(lhs, ck_ref[0].astype(BF16)) + bias2[:, :past]
    s_n = _nt_dot(lhs, kn_ref[...]) + bias2[:, past:]
    m = jnp.maximum(jnp.max(s_c, axis=1, keepdims=True), jnp.max(s_n, axis=1, keepdims=True))
    p_c = jnp.exp(s_c - m)
    p_n = jnp.exp(s_n - m)
    l = jnp.sum(p_c, axis=1, keepdims=True) + jnp.sum(p_n, axis=1, keepdims=True)
    acc = (jnp.dot(p_c.astype(BF16), cv_ref[0].astype(BF16), preferred_element_type=F32)
           + jnp.dot(p_n.astype(BF16), vn_ref[...], preferred_element_type=F32))
    o = _diff_finish(acc, l, lam_ref[0], g_ref[...], tq, post_scale)
    o_ref[...] = o.astype(o_ref.dtype)


def _attn_sample(qb, kb, vb, cache_k, cache_v, rel_bias, lam, g_row, post_scale):
    nstream, past, width = cache_k.shape
    nh = width // LANES
    ls = qb.shape[0] // nstream
    no_shift = jnp.zeros((nh,), F32)
    table = jnp.swapaxes(_bias_table(rel_bias, past + ls, ls, past, no_shift), 1, 2)
    new = lambda b, h: (b, h)
    old = lambda b, h: (b, 0, h)
    return pl.pallas_call(
        functools.partial(_sample_attn_kernel, past=past, post_scale=post_scale),
        grid=(nstream, nh),
        in_specs=[
            pl.BlockSpec(memory_space=pltpu.SMEM),
            pl.BlockSpec((ls, LANES), new),
            pl.BlockSpec((1, past, LANES), old),
            pl.BlockSpec((1, past, LANES), old),
            pl.BlockSpec((ls, LANES), new),
            pl.BlockSpec((ls, LANES), new),
            pl.BlockSpec((1, ls, past + ls), lambda b, h: (h, 0, 0)),
            pl.BlockSpec((1, LANES), lambda b, h: (0, 0)),
        ],
        out_specs=pl.BlockSpec((ls, LANES), new),
        out_shape=jax.ShapeDtypeStruct(qb.shape, BF16),
        compiler_params=_cparams(("arbitrary", "arbitrary")),
    )(lam, qb, cache_k, cache_v, kb, vb, table, g_row)


def _mix_kernel(x_ref, y_ref, a_ref, wglu_ref, bglu_ref, wout_ref, g1_ref, b1_ref,
                wrh_ref, wrl_ref, br_ref, h_ref, lg_ref):
    half = y_ref.shape[1]
    g = jax.nn.gelu(y_ref[...])
    t = jnp.dot(g.astype(BF16), wglu_ref[...], preferred_element_type=F32) + bglu_ref[...]
    so = g * jax.nn.sigmoid(t)
    mix = (jnp.dot(so.astype(BF16), wout_ref[0:half, :], preferred_element_type=F32)
           + jnp.dot(a_ref[...], wout_ref[half:2 * half, :], preferred_element_type=F32))
    h = _layer_norm(DN_ALPHA * x_ref[...] + mix, g1_ref[...], b1_ref[...])
    h_ref[...] = h
    h_hi = h.astype(BF16)
    h_lo = (h - h_hi.astype(F32)).astype(BF16)
    lg = (jnp.dot(h_hi, wrh_ref[...], preferred_element_type=F32)
          + jnp.dot(h_lo, wrh_ref[...], preferred_element_type=F32)
          + jnp.dot(h_hi, wrl_ref[...], preferred_element_type=F32))
    lg_ref[...] = lg + br_ref[...]


def _mix(x2d, y_ssm, attn, wglu_b, bglu, wout_b, g1, b1, wr_hi, wr_lo, br, tm):
    t, d = x2d.shape
    half = d // 2
    row = lambda i: (i, 0)
    fix = lambda i: (0, 0)
    once = pl.Buffered(1)
    return pl.pallas_call(
        _mix_kernel,
        grid=(t // tm,),
        in_specs=[
            pl.BlockSpec((tm, d), row),
            pl.BlockSpec((tm, half), row),
            pl.BlockSpec((tm, half), row),
            pl.BlockSpec((half, half), fix, pipeline_mode=once),
            pl.BlockSpec((1, half), fix),
            pl.BlockSpec((d, d), fix, pipeline_mode=once),
            pl.BlockSpec((1, d), fix),
            pl.BlockSpec((1, d), fix),
            pl.BlockSpec((d, LANES), fix),
            pl.BlockSpec((d, LANES), fix),
            pl.BlockSpec((1, LANES), fix),
        ],
        out_specs=[pl.BlockSpec((tm, d), row), pl.BlockSpec((tm, LANES), row)],
        out_shape=[jax.ShapeDtypeStruct((t, d), F32), jax.ShapeDtypeStruct((t, LANES), F32)],
        compiler_params=_cparams(("arbitrary",)),
    )(x2d, y_ssm, attn, wglu_b, bglu, wout_b, g1, b1, wr_hi, wr_lo, br)


def _route_kernel(lg_ref, dest_ref, gate_ref, cnt_ref, tot_scr, run_scr, start_scr):
    ph = pl.program_id(0)
    i = pl.program_id(1)
    tm = lg_ref.shape[0]
    ng = N_EXPERT_GROUPS

    @pl.when((ph == 0) & (i == 0))
    def _():
        tot_scr[...] = jnp.zeros_like(tot_scr)

    lg = lg_ref[...]
    lane = lax.broadcasted_iota(jnp.int32, lg.shape, 1)
    is_g = lane < ng
    l1 = jnp.where(is_g, lg, -jnp.inf)
    m1 = jnp.max(l1, axis=1, keepdims=True)
    z1 = jnp.sum(jnp.where(is_g, jnp.exp(l1 - m1), 0.0), axis=1, keepdims=True)
    w_grp = 1.0 / z1
    el = (lane - ng).astype(F32)
    none = float(LANES)
    grp = jnp.min(jnp.where(l1 == m1, lane.astype(F32), none), axis=1, keepdims=True)
    lo = grp * EXPERTS_PER_GROUP
    in_grp = jnp.where(el >= lo, jnp.where(el < lo + EXPERTS_PER_GROUP, 1.0, 0.0), 0.0) > 0.5
    l2 = jnp.where(in_grp, lg, -jnp.inf)
    t1 = jnp.max(l2, axis=1, keepdims=True)
    i1 = jnp.min(jnp.where(l2 == t1, el, none), axis=1, keepdims=True)
    l2b = jnp.where(el == i1, -jnp.inf, l2)
    t2 = jnp.max(l2b, axis=1, keepdims=True)
    i2 = jnp.min(jnp.where(l2b == t2, el, none), axis=1, keepdims=True)
    e2 = jnp.exp(t2 - t1)
    den = 1.0 + e2
    g0 = w_grp * (1.0 / den)
    g1 = w_grp * (e2 / den)
    hit1 = el == i1
    hit2 = el == i2
    oh = jnp.where(hit1, 1.0, jnp.where(hit2, 1.0, 0.0))

    @pl.when(ph == 0)
    def _():
        tot_scr[...] = tot_scr[...] + jnp.sum(oh, axis=0, keepdims=True)

    @pl.when(ph == 1)
    def _():
        @pl.when(i == 0)
        def _():
            cnt = tot_scr[...]
            padded = jnp.floor((cnt + (MOE_ROWS - 1)) * (1.0 / MOE_ROWS)) * MOE_ROWS
            padded = jnp.broadcast_to(padded, (8, LANES))
            lane8 = lax.broadcasted_iota(jnp.int32, padded.shape, 1)
            ends = padded
            for sh in (1, 2, 4, 8, 16, 32):
                ends = ends + jnp.where(lane8 >= sh, pltpu.roll(ends, sh, axis=1), 0.0)
            start_scr[...] = (ends - padded)[0:1]
            run_scr[...] = jnp.zeros_like(run_scr)
            cnt_ref[...] = cnt

        r_id = lax.broadcasted_iota(jnp.int32, (tm, tm), 0)
        c_id = lax.broadcasted_iota(jnp.int32, (tm, tm), 1)
        tri = jnp.where(c_id < r_id, 1.0, 0.0).astype(BF16)
        place = (jnp.dot(tri, oh.astype(BF16), preferred_element_type=F32)
                 + (run_scr[...] + start_scr[...]))
        d1 = jnp.sum(jnp.where(hit1, place, 0.0), axis=1, keepdims=True)
        d2 = jnp.sum(jnp.where(hit2, place, 0.0), axis=1, keepdims=True)
        run_scr[...] = run_scr[...] + jnp.sum(oh, axis=0, keepdims=True)
        dmat = jnp.where(lane == 0, d1, jnp.where(lane == 1, d2, 0.0))
        dest_ref[0] = dmat.T[0:8, :].astype(jnp.int32)
        gate_ref[...] = jnp.where(lane == 0, g0, jnp.where(lane == 1, g1, 0.0))


def _route(logits, tm):
    t = logits.shape[0]
    placed = lambda ph, i: (ph * i, 0)
    return pl.pallas_call(
        _route_kernel,
        grid=(2, t // tm),
        in_specs=[pl.BlockSpec((tm, LANES), lambda ph, i: (i, 0))],
        out_specs=[
            pl.BlockSpec((1, 8, tm), lambda ph, i: (ph * i, 0, 0)),
            pl.BlockSpec((tm, LANES), placed),
            pl.BlockSpec((1, LANES), lambda ph, i: (0, 0)),
        ],
        out_shape=[
            jax.ShapeDtypeStruct((t // tm, 8, tm), jnp.int32),
            jax.ShapeDtypeStruct((t, LANES), F32),
            jax.ShapeDtypeStruct((1, LANES), F32),
        ],
        scratch_shapes=[pltpu.VMEM((1, LANES), F32)] * 3,
        compiler_params=_cparams(("arbitrary", "arbitrary")),
    )(logits)


def _row_copy(src, s_row, dst, d_row, sem):
    return pltpu.make_async_copy(src.at[pl.ds(s_row, 1), :], dst.at[pl.ds(d_row, 1), :], sem)


def _dispatch_kernel(dest_ref, h_ref, xin_ref, xbuf_ref, sem):
    del xin_ref
    tm = h_ref.shape[0]

    def issue(r, c):
        _row_copy(h_ref, r, xbuf_ref, dest_ref[0, 0, r], sem).start()
        _row_copy(h_ref, r, xbuf_ref, dest_ref[0, 1, r], sem).start()
        return c

    lax.fori_loop(0, tm, issue, 0, unroll=DMA_UNROLL)

    def drain(r, c):
        _row_copy(h_ref, 0, xbuf_ref, 0, sem).wait()
        _row_copy(h_ref, 0, xbuf_ref, 0, sem).wait()
        return c

    lax.fori_loop(0, tm, drain, 0, unroll=DMA_UNROLL)


def _dest_spec(tm_route, tm):
    per = tm_route // tm
    return pl.BlockSpec((1, 8, tm), lambda i: (i // per, 0, i % per), memory_space=pltpu.SMEM)


def _dispatch(h, dest, n_rows, tm):
    t, d = h.shape
    xbuf0 = jnp.zeros((n_rows, d), h.dtype)
    return pl.pallas_call(
        _dispatch_kernel,
        grid=(t // tm,),
        in_specs=[
            _dest_spec(dest.shape[2], tm),
            pl.BlockSpec((tm, d), lambda i: (i, 0)),
            pl.BlockSpec(memory_space=pl.ANY),
        ],
        out_specs=pl.BlockSpec(memory_space=pl.ANY),
        out_shape=jax.ShapeDtypeStruct((n_rows, d), h.dtype),
        scratch_shapes=[pltpu.SemaphoreType.DMA(())],
        input_output_aliases={2: 0},
        compiler_params=_cparams(("arbitrary",)),
    )(dest, h, xbuf0)


def _expert_kernel(be_ref, nused_ref, x_ref, wg_ref, wu_ref, wd_ref, y_ref):
    i = pl.program_id(0)

    @pl.when(i < nused_ref[0])
    def _():
        x = x_ref[...].astype(BF16)
        gt = jnp.dot(x, wg_ref[0], preferred_element_type=F32)
        up = jnp.dot(x, wu_ref[0], preferred_element_type=F32)
        a = (jax.nn.silu(gt) * up).astype(BF16)
        y_ref[...] = jnp.dot(a, wd_ref[0], preferred_element_type=F32)

    @pl.when(i >= nused_ref[0])
    def _():
        y_ref[...] = jnp.zeros_like(y_ref)


def _experts(xbuf, block_e, n_used, wg_b, wu_b, wd_b):
    n_rows, d = xbuf.shape
    de = wg_b.shape[-1]
    nb = n_rows // MOE_ROWS
    wmap = lambda i, be, nu: (be[i], 0, 0)
    grid_spec = pltpu.PrefetchScalarGridSpec(
        num_scalar_prefetch=2,
        grid=(nb,),
        in_specs=[
            pl.BlockSpec((MOE_ROWS, d), lambda i, be, nu: (i, 0)),
            pl.BlockSpec((1, d, de), wmap),
            pl.BlockSpec((1, d, de), wmap),
            pl.BlockSpec((1, de, d), wmap),
        ],
        out_specs=pl.BlockSpec((MOE_ROWS, d), lambda i, be, nu: (i, 0)),
    )
    return pl.pallas_call(
        _expert_kernel,
        grid_spec=grid_spec,
        out_shape=jax.ShapeDtypeStruct((n_rows, d), F32),
        compiler_params=_cparams(("arbitrary",)),
    )(block_e, n_used, xbuf, wg_b, wu_b, wd_b)


def _combine_kernel(dest_ref, h_ref, gate_ref, g2_ref, b2_ref, ybuf_ref, o_ref, ya_scr, yb_scr, sem):
    tm = h_ref.shape[0]

    def issue(r, c):
        _row_copy(ybuf_ref, dest_ref[0, 0, r], ya_scr, r, sem).start()
        _row_copy(ybuf_ref, dest_ref[0, 1, r], yb_scr, r, sem).start()
        return c

    lax.fori_loop(0, tm, issue, 0, unroll=DMA_UNROLL)

    def drain(r, c):
        _row_copy(ybuf_ref, 0, ya_scr, 0, sem).wait()
        _row_copy(ybuf_ref, 0, yb_scr, 0, sem).wait()
        return c

    lax.fori_loop(0, tm, drain, 0, unroll=DMA_UNROLL)
    gate = gate_ref[...]
    f = ya_scr[...] * gate[:, 0:1] + yb_scr[...] * gate[:, 1:2]
    o_ref[...] = _layer_norm(DN_ALPHA * h_ref[...] + f, g2_ref[...], b2_ref[...])


def _combine(h, gate, dest, ybuf, g2, b2, tm):
    t, d = h.shape
    row = lambda i: (i, 0)
    fix = lambda i: (0, 0)
    return pl.pallas_call(
        _combine_kernel,
        grid=(t // tm,),
        in_specs=[
            _dest_spec(dest.shape[2], tm),
            pl.BlockSpec((tm, d), row),
            pl.BlockSpec((tm, LANES), row),
            pl.BlockSpec((1, d), fix),
            pl.BlockSpec((1, d), fix),
            pl.BlockSpec(memory_space=pl.ANY),
        ],
        out_specs=pl.BlockSpec((tm, d), row),
        out_shape=jax.ShapeDtypeStruct((t, d), F32),
        scratch_shapes=[pltpu.VMEM((tm, d), F32), pltpu.VMEM((tm, d), F32), pltpu.SemaphoreType.DMA(())],
        compiler_params=_cparams(("arbitrary",)),
    )(dest, h, gate, g2, b2, ybuf)


def _moe_finish(h, logits, wg_b, wu_b, wd_b, g2, b2, tm_route, tm_rows):
    t = h.shape[0]
    dest, gate, cnt = _route(logits, tm_route)
    counts = cnt[0, N_EXPERT_GROUPS:N_EXPERT_GROUPS + N_EXPERTS].astype(jnp.int32)
    pad_end = jnp.cumsum((counts + MOE_ROWS - 1) // MOE_ROWS * MOE_ROWS)
    n_blocks = -(-(2 * t) // MOE_ROWS) + N_EXPERTS
    first_row = jnp.arange(n_blocks, dtype=jnp.int32)[:, None] * MOE_ROWS
    block_e = jnp.minimum(jnp.sum((pad_end[None, :] <= first_row).astype(jnp.int32), axis=1),
                          N_EXPERTS - 1)
    n_used = (pad_end[-1:] // MOE_ROWS).astype(jnp.int32)
    xbuf = _dispatch(h, dest, n_blocks * MOE_ROWS, tm_rows)
    ybuf = _experts(xbuf, block_e, n_used, wg_b, wu_b, wd_b)
    return _combine(h, gate, dest, ybuf, g2, b2, tm_rows)


def _pick(t, pref):
    tm = min(pref, t)
    while t % tm:
        tm //= 2
    return tm


def _stream(x, h0_re, h0_im, cache_k, cache_v, params, lam, lam_init):
    (w_in_b, tables, d_row, wglu_b, bglu, rel_bias, g_row, wout_b, g1, b1,
     wr_hi, wr_lo, br, wg_b, wu_b, wd_b, g2, b2) = params
    b, l, d = x.shape
    t = b * l
    x2d = x.reshape(t, d)
    u, k, v, qb, kb, vb, qt, vt = _in_proj(x2d, w_in_b, _pick(t, 512))
    y_ssm, h_re, h_im = _ssm(u, tables, d_row, h0_re, h0_im, l, _pick(t, 4096))
    post = 1.0 - lam_init
    if cache_k is None:
        attn = _attn_prompt(qt, kb, vt, rel_bias, lam, g_row, b, l, post, _pick(l, ATTN_TILE))
    else:
        attn = _attn_sample(qb, kb, vb, cache_k, cache_v, rel_bias, lam, g_row, post)
    h, logits = _mix(x2d, y_ssm, attn, wglu_b, bglu, wout_b, g1, b1, wr_hi, wr_lo, br, _pick(t, 256))
    out = _moe_finish(h, logits, wg_b, wu_b, wd_b, g2, b2, _pick(t, 512), _pick(t, 256))
    return out.reshape(b, l, d), k, v, h_re, h_im


def kernel(x_prompt, x_sample, cache_k, cache_v, state_ssm_re, state_ssm_im, w_in, ssm_a_re, ssm_a_im, ssm_log_dt, ssm_b_re, ssm_b_im, ssm_c_re, ssm_c_im, ssm_d, w_glu, b_glu, lambda_q1, lambda_k1, lambda_q2, lambda_k2, subln_g, rel_bias, w_out, ln1_g, ln1_b, w_r1, b_r1, w_r2, b_r2, w_gate, w_up, w_down, ln2_g, ln2_b):
    assert w_in.shape[0] == DEPTH
    bp, lp, d = x_prompt.shape
    bs, ls, _ = x_sample.shape
    past = cache_k.shape[2]
    nh, dqk = cache_k.shape[3], cache_k.shape[4]
    n_grp, n_state = state_ssm_re.shape[2], state_ssm_re.shape[3]
    l = 0
    lam_init = 0.8 - 0.6 * math.exp(-0.3 * l)
    lam = (jnp.exp(jnp.sum(lambda_q1[l].astype(F32) * lambda_k1[l].astype(F32)))
           - jnp.exp(jnp.sum(lambda_q2[l].astype(F32) * lambda_k2[l].astype(F32))) + lam_init).reshape(1)
    tables = _ssm_tables(ssm_a_re[l].astype(F32), ssm_a_im[l].astype(F32), ssm_log_dt[l].astype(F32),
                         ssm_b_re[l].astype(F32), ssm_b_im[l].astype(F32),
                         ssm_c_re[l].astype(F32), ssm_c_im[l].astype(F32))
    wr = jnp.concatenate([w_r1[l].astype(F32), w_r2[l].astype(F32).reshape(d, N_EXPERTS)], axis=1)
    wr = jnp.pad(wr, ((0, 0), (0, LANES - wr.shape[1])))
    wr_hi = wr.astype(BF16)
    wr_lo = (wr - wr_hi.astype(F32)).astype(BF16)
    br = jnp.concatenate([b_r1[l].astype(F32), b_r2[l].astype(F32).reshape(N_EXPERTS)])
    br = jnp.pad(br, (0, LANES - br.shape[0])).reshape(1, LANES)
    params = (
        w_in[l].astype(BF16), tables, ssm_d[l].astype(F32).reshape(1, -1),
        w_glu[l].astype(BF16), b_glu[l].astype(F32).reshape(1, -1), rel_bias,
        subln_g[l].astype(F32).reshape(1, -1), w_out[l].astype(BF16),
        ln1_g[l].astype(F32).reshape(1, -1), ln1_b[l].astype(F32).reshape(1, -1),
        wr_hi, wr_lo, br,
        w_gate[l].astype(BF16), w_up[l].astype(BF16), w_down[l].astype(BF16),
        ln2_g[l].astype(F32).reshape(1, -1), ln2_b[l].astype(F32).reshape(1, -1),
    )
    zeros = jnp.zeros((bp, n_grp * n_state), F32)
    yp, kp, vp, hp_re, hp_im = _stream(x_prompt, zeros, zeros, None, None, params, lam, lam_init)
    ys, ks, vs, hs_re, hs_im = _stream(
        x_sample, state_ssm_re[l].astype(F32).reshape(bs, -1), state_ssm_im[l].astype(F32).reshape(bs, -1),
        cache_k[l].reshape(bs, past, nh * dqk), cache_v[l].reshape(bs, past, -1), params, lam, lam_init)
    return (yp, ys,
            kp.reshape(1, bp, lp, nh, dqk).astype(cache_k.dtype),
            vp.reshape(1, bp, lp, nh, -1).astype(cache_v.dtype),
            hp_re.reshape(1, bp, n_grp, n_state).astype(state_ssm_re.dtype),
            hp_im.reshape(1, bp, n_grp, n_state).astype(state_ssm_im.dtype),
            ks.reshape(1, bs, ls, nh, dqk).astype(cache_k.dtype),
            vs.reshape(1, bs, ls, nh, -1).astype(cache_v.dtype),
            hs_re.reshape(1, bs, n_grp, n_state).astype(state_ssm_re.dtype),
            hs_im.reshape(1, bs, n_grp, n_state).astype(state_ssm_im.dtype))
```

```python
import functools
import math

import jax
import jax.numpy as jnp
from jax import lax
from jax.experimental import pallas as pl
from jax.experimental.pallas import tpu as pltpu

F32 = jnp.float32
BF16 = jnp.bfloat16

CHUNK = 64
SSM_GROUP_CH = 16
SSM_STATE = 64
N_ATTN_HEADS = 8
ATTN_HEAD_DIM = 64
REL_BUCKETS = 32
REL_MAX_DIST = 128
N_EXPERT_GROUPS = 4
EXPERTS_PER_GROUP = 8
N_EXPERTS = N_EXPERT_GROUPS * EXPERTS_PER_GROUP
DEPTH = 1
DN_ALPHA = (2 * DEPTH) ** 0.25
LN_EPS = 1e-5
RMS_EPS = 1e-5
NEG_INF = -1e30
LOG2E = math.log2(math.e)

LANES = 128
VMEM_LIMIT = 56 * 1024 * 1024

SSM_SUB = 16
SSM_OCT = LANES // SSM_GROUP_CH
ATTN_TILE = 1024
ATTN_COLS = 256
MOE_ROWS = 256
DMA_UNROLL = 8


def _cparams(sem, vmem=VMEM_LIMIT):
    return pltpu.CompilerParams(dimension_semantics=sem, vmem_limit_bytes=vmem)


def _layer_norm(r, g, b):
    mu = jnp.mean(r, axis=-1, keepdims=True)
    rc = r - mu
    var = jnp.mean(rc * rc, axis=-1, keepdims=True)
    return rc * lax.rsqrt(var + LN_EPS) * g + b


def _store_heads_transposed(dst_ref, z):
    for h in range(dst_ref.shape[0]):
        dst_ref[h] = z[:, LANES * h:LANES * (h + 1)].T.astype(BF16)


def _store_heads_split(dst_ref, z):
    for h in range(dst_ref.shape[1]):
        dst_ref[:, h, :] = z[:, LANES * h:LANES * (h + 1)]


def _in_proj_kernel(x_ref, w_ref, u_ref, k_ref, v_ref, qb_ref, kb_ref, vb_ref, qt_ref, vt_ref, xb_scr,
                    *, q_scale):
    j = pl.program_id(1)

    @pl.when(j == 0)
    def _():
        xb_scr[...] = x_ref[...].astype(BF16)

    z = jnp.dot(xb_scr[...], w_ref[...], preferred_element_type=F32)

    @pl.when(j == 0)
    def _():
        u_ref[...] = z

    @pl.when(j == 1)
    def _():
        zq = z * q_scale
        qb_ref[...] = zq.astype(BF16)
        _store_heads_transposed(qt_ref, zq * LOG2E)

    @pl.when(j == 2)
    def _():
        _store_heads_split(k_ref, z)
        kb_ref[...] = z.astype(BF16)

    @pl.when(j == 3)
    def _():
        _store_heads_split(v_ref, z)
        vb_ref[...] = z.astype(BF16)
        _store_heads_transposed(vt_ref, z)


def _in_proj(x2d, w_in_b, tm):
    t, d = x2d.shape
    n = w_in_b.shape[1] // 4
    nh = n // LANES
    row = lambda i, j: (i, 0)
    f32_out = jax.ShapeDtypeStruct((t, n), F32)
    b16_out = jax.ShapeDtypeStruct((t, n), BF16)
    tr_out = jax.ShapeDtypeStruct((nh, LANES, t), BF16)
    tr_spec = pl.BlockSpec((nh, LANES, tm), lambda i, j: (0, 0, i))
    cache_out = jax.ShapeDtypeStruct((t, nh, LANES), F32)
    cache_spec = pl.BlockSpec((tm, nh, LANES), lambda i, j: (i, 0, 0))
    return pl.pallas_call(
        functools.partial(_in_proj_kernel, q_scale=ATTN_HEAD_DIM ** -0.5),
        grid=(t // tm, 4),
        in_specs=[pl.BlockSpec((tm, d), row), pl.BlockSpec((d, n), lambda i, j: (0, j))],
        out_specs=[pl.BlockSpec((tm, n), row), cache_spec, cache_spec] + [pl.BlockSpec((tm, n), row)] * 3
        + [tr_spec] * 2,
        out_shape=[f32_out, cache_out, cache_out, b16_out, b16_out, b16_out, tr_out, tr_out],
        scratch_shapes=[pltpu.VMEM((tm, d), BF16)],
        compiler_params=_cparams(("arbitrary", "arbitrary")),
    )(x2d, w_in_b)


def _ssm_tables(a_re, a_im, log_dt, b_re, b_im, c_re, c_im):
    hp = lax.Precision.HIGHEST
    g, p = a_re.shape
    c = b_re.shape[-1]
    s, o = SSM_SUB, SSM_OCT
    dt = jnp.exp(log_dt)[:, None]
    mag = jnp.exp(a_re * dt)
    ab_re = mag * jnp.cos(a_im * dt)
    ab_im = mag * jnp.sin(a_im * dt)
    nr = ab_re - 1.0
    ni = ab_im
    den = a_re * a_re + a_im * a_im
    z_re = ((nr * a_re + ni * a_im) / den)[..., None]
    z_im = ((ni * a_re - nr * a_im) / den)[..., None]
    bb_re = z_re * b_re - z_im * b_im
    bb_im = z_re * b_im + z_im * b_re
    n = jnp.arange(s + 1, dtype=F32)[:, None, None]
    pmag = jnp.exp(n * (a_re * dt))
    pw_re = pmag * jnp.cos(n * (a_im * dt))
    pw_im = pmag * jnp.sin(n * (a_im * dt))
    ca_re = c_re[None] * pw_re[:, :, None, :] - c_im[None] * pw_im[:, :, None, :]
    ca_im = c_re[None] * pw_im[:, :, None, :] + c_im[None] * pw_re[:, :, None, :]
    kern = (jnp.einsum('dgop,gpi->dgio', ca_re[:s], bb_re, precision=hp)
            - jnp.einsum('dgop,gpi->dgio', ca_im[:s], bb_im, precision=hp))
    n_oct = g // o

    def expand_groups(tbl, col_group):
        j, n = tbl.shape[1], tbl.shape[3]
        full = jnp.broadcast_to(tbl[:, :, None], (n_oct, j, o, c, n)).reshape(n_oct, j * o * c, n)
        row_l = (lax.broadcasted_iota(jnp.int32, (j * o * c, n), 0) // c) % o
        col_l = col_group(lax.broadcasted_iota(jnp.int32, (j * o * c, n), 1))
        return jnp.where((row_l == col_l)[None], full, 0.0).astype(BF16)

    kt = jnp.transpose(kern.reshape(s, n_oct, o, c, c), (1, 3, 0, 2, 4))
    lags = expand_groups(kt.reshape(n_oct, 1, c, s * o * c), lambda col: (col % LANES) // c)
    m = jnp.concatenate(
        [jnp.pad(lags[:, :, :LANES * (s - j)], ((0, 0), (0, 0), (LANES * j, 0))) for j in range(s)],
        axis=1)
    nrev = (s - 1) - jnp.arange(s, dtype=F32)[:, None, None]
    rmag = jnp.exp(nrev * (a_re * dt))
    rv_re = rmag * jnp.cos(nrev * (a_im * dt))
    rv_im = rmag * jnp.sin(nrev * (a_im * dt))
    wb_re = rv_re[..., None] * bb_re[None] - rv_im[..., None] * bb_im[None]
    wb_im = rv_re[..., None] * bb_im[None] + rv_im[..., None] * bb_re[None]
    state_group = lambda col: (col % (o * p)) // p
    wb = jnp.stack([wb_re, wb_im], axis=0).reshape(2, s, n_oct, o, p, c)
    wt = jnp.transpose(wb, (2, 1, 5, 0, 3, 4)).reshape(n_oct, s, c, 2 * o * p)
    w = expand_groups(wt, state_group)
    zc = jnp.stack([ca_re[1:], -ca_im[1:]], axis=0).reshape(2, s, n_oct, o, c, p)
    zt = jnp.transpose(zc, (2, 1, 4, 0, 3, 5)).reshape(n_oct, s, c, 2 * o * p)
    z_t = expand_groups(zt, state_group)
    a_s_re = pw_re[s].reshape(n_oct, 1, o * p)
    a_s_im = pw_im[s].reshape(n_oct, 1, o * p)
    return m, w, z_t, a_s_re, a_s_im


def _ssm_kernel(u_ref, m_ref, w_ref, zt_ref, are_ref, aim_ref, d_ref, h0re_ref, h0im_ref,
                y_ref, hre_ref, him_ref, lhs_scr, v_scr, sin_scr, cre_scr, cim_scr, *, rows, cps):
    i = pl.program_id(1)
    half = cre_scr.shape[1]

    @pl.when(i == 0)
    def _():
        cre_scr[...] = jnp.zeros_like(cre_scr)
        cim_scr[...] = jnp.zeros_like(cim_scr)

    for j in range(SSM_SUB):
        piece = u_ref[pl.ds(j, rows, stride=SSM_SUB), :]
        lhs_scr[:, LANES * j:LANES * (j + 1)] = piece.astype(BF16)
    lhs = lhs_scr[...]
    v_scr[...] = jnp.dot(lhs, w_ref[0], preferred_element_type=F32)
    a_re = are_ref[0]
    a_im = aim_ref[0]

    def body(r, carry):
        s_re, s_im = carry
        idx = i * rows + r
        seq = idx // cps
        start = (idx % cps) == 0
        s_re = jnp.where(start, h0re_ref[pl.ds(seq, 1), :], s_re)
        s_im = jnp.where(start, h0im_ref[pl.ds(seq, 1), :], s_im)
        sin_scr[pl.ds(r, 1), 0:half] = s_re
        sin_scr[pl.ds(r, 1), half:2 * half] = s_im
        v_re = v_scr[pl.ds(r, 1), 0:half]
        v_im = v_scr[pl.ds(r, 1), half:2 * half]
        n_re = a_re * s_re - a_im * s_im + v_re
        n_im = a_re * s_im + a_im * s_re + v_im
        hre_ref[pl.ds(seq, 1), :] = n_re
        him_ref[pl.ds(seq, 1), :] = n_im
        return n_re, n_im

    s_re, s_im = lax.fori_loop(0, rows, body, (cre_scr[...], cim_scr[...]))
    cre_scr[...] = s_re
    cim_scr[...] = s_im

    y = (jnp.dot(lhs, m_ref[0], preferred_element_type=F32)
         + lax.dot_general(sin_scr[...].astype(BF16), zt_ref[0], (((1,), (1,)), ((), ())),
                           preferred_element_type=F32))
    d = d_ref[...]
    for j in range(SSM_SUB):
        uj = u_ref[pl.ds(j, rows, stride=SSM_SUB), :]
        y_ref[pl.ds(j, rows, stride=SSM_SUB), :] = y[:, LANES * j:LANES * (j + 1)] + d * uj


def _ssm(u, tables, d_row, h0_re, h0_im, seq_len, tm):
    m, w, z, a_re, a_im = tables
    t, width = u.shape
    n_oct = width // LANES
    nseq = h0_re.shape[0]
    rows = tm // SSM_SUB
    half = a_re.shape[-1]
    kdim = SSM_SUB * LANES
    once = pl.Buffered(1)
    col = lambda g, i: (i, g)
    per_oct3 = lambda g, i: (g, 0, 0)
    per_oct2 = lambda g, i: (0, g)
    return pl.pallas_call(
        functools.partial(_ssm_kernel, rows=rows, cps=seq_len // SSM_SUB),
        grid=(n_oct, t // tm),
        in_specs=[
            pl.BlockSpec((tm, LANES), col),
            pl.BlockSpec((1, kdim, kdim), per_oct3, pipeline_mode=once),
            pl.BlockSpec((1, kdim, 2 * half), per_oct3, pipeline_mode=once),
            pl.BlockSpec((1, kdim, 2 * half), per_oct3, pipeline_mode=once),
            pl.BlockSpec((1, 1, half), per_oct3),
            pl.BlockSpec((1, 1, half), per_oct3),
            pl.BlockSpec((1, LANES), per_oct2),
            pl.BlockSpec((nseq, half), per_oct2),
            pl.BlockSpec((nseq, half), per_oct2),
        ],
        out_specs=[
            pl.BlockSpec((tm, LANES), col),
            pl.BlockSpec((nseq, half), per_oct2),
            pl.BlockSpec((nseq, half), per_oct2),
        ],
        out_shape=[
            jax.ShapeDtypeStruct((t, width), F32),
            jax.ShapeDtypeStruct((nseq, n_oct * half), F32),
            jax.ShapeDtypeStruct((nseq, n_oct * half), F32),
        ],
        scratch_shapes=[
            pltpu.VMEM((rows, kdim), BF16),
            pltpu.VMEM((rows, 2 * half), F32),
            pltpu.VMEM((rows, 2 * half), F32),
            pltpu.VMEM((1, half), F32),
            pltpu.VMEM((1, half), F32),
        ],
        compiler_params=_cparams(("arbitrary", "arbitrary")),
    )(u, m, w, z, a_re, a_im, d_row, h0_re, h0_im)


def _rel_bucket(rel):
    nb = REL_BUCKETS // 2
    max_exact = nb // 2
    n = jnp.abs(rel)
    nf = jnp.maximum(n, 1).astype(F32)
    large = max_exact + (jnp.log(nf / max_exact) / math.log(REL_MAX_DIST / max_exact)
                         * (nb - max_exact)).astype(jnp.int32)
    large = jnp.minimum(large, nb - 1)
    return jnp.where(rel > 0, nb, 0) + jnp.where(n < max_exact, n, large)


def _bias_table(rel_bias, n_k, n_q, q0, shift):
    k_pos = lax.broadcasted_iota(jnp.int32, (n_k, n_q), 0)
    q_pos = q0 + lax.broadcasted_iota(jnp.int32, (n_k, n_q), 1)
    bucket = _rel_bucket(k_pos - q_pos)[None]
    rows = (rel_bias.astype(F32) - shift[None, :])[:, :, None, None]
    bias = jnp.broadcast_to(rows[0], (rel_bias.shape[1], n_k, n_q))
    for b in range(1, REL_BUCKETS):
        bias = jnp.where(bucket == b, rows[b], bias)
    return jnp.where((k_pos // CHUNK <= q_pos // CHUNK)[None], bias, NEG_INF)


def _split_heads_lhs(q):
    lane = lax.broadcasted_iota(jnp.int32, q.shape, 1)
    zero = jnp.zeros_like(q)
    return jnp.concatenate([jnp.where(lane < ATTN_HEAD_DIM, q, zero),
                            jnp.where(lane >= ATTN_HEAD_DIM, q, zero)], axis=0)


def _nt_dot(a, b):
    return lax.dot_general(a, b, (((1,), (1,)), ((), ())), preferred_element_type=F32)


def _diff_finish(acc, l, lam, g, tq, post_scale):
    o = acc[:tq] / l[:tq] - lam * (acc[tq:] / l[tq:])
    ms = jnp.mean(o * o, axis=-1, keepdims=True)
    return o * lax.rsqrt(ms + RMS_EPS) * g * post_scale


def _flash_kernel(lam_ref, qt_ref, k_ref, vt_ref, bias_ref, g_ref, o_ref,
                  lhs_scr, s0_scr, s1_scr, m_scr, acc_scr, *, tq, tk, cb, post_scale):
    qi = pl.program_id(2)
    qt = qt_ref[0]
    row = lax.broadcasted_iota(jnp.int32, qt.shape, 0)
    zero = jnp.zeros_like(qt)
    lhs_scr[:, 0:tq] = jnp.where(row < ATTN_HEAD_DIM, qt, zero)
    lhs_scr[:, tq:2 * tq] = jnp.where(row >= ATTN_HEAD_DIM, qt, zero)
    m_scr[...] = jnp.full_like(m_scr, NEG_INF)
    acc_scr[...] = jnp.zeros_like(acc_scr)
    ones_rows = jnp.ones((acc_scr.shape[0] - LANES, tk), BF16)
    s_bufs = (s0_scr, s1_scr)

    all_blocks = tuple(range(2 * tq // cb))
    late_blocks = tuple(c for c in all_blocks if (c * cb) % tq >= tk)

    def qk(tile, buf, blocks=None):
        k = k_ref[pl.ds(pl.multiple_of(tile * tk, tk), tk), :]
        if blocks is None:
            s_bufs[buf][...] = jnp.dot(k, lhs_scr[...], preferred_element_type=F32)
        else:
            for c in blocks:
                cols = pl.ds(c * cb, cb)
                s_bufs[buf][:, cols] = jnp.dot(k, lhs_scr[:, cols], preferred_element_type=F32)

    def softmax_pv(tile, buf, bias, blocks=all_blocks):
        vt = vt_ref[0, :, pl.ds(pl.multiple_of(tile * tk, tk), tk)]
        vt = jnp.concatenate([vt, ones_rows], axis=0)
        for c in blocks:
            cols = pl.ds(c * cb, cb)
            s = s_bufs[buf][:, cols]
            if bias is not None:
                b0 = (c * cb) % tq
                s = s + bias[:, b0:b0 + cb]
            m_prev = m_scr[:, cols]
            m_new = jnp.maximum(m_prev, jnp.max(s, axis=0, keepdims=True))
            alpha = jnp.exp2(m_prev - m_new)
            p = jnp.exp2(s - m_new).astype(BF16)
            acc_scr[:, cols] = alpha * acc_scr[:, cols] + jnp.dot(vt, p, preferred_element_type=F32)
            m_scr[:, cols] = m_new

    qk(0, 0)

    def far_pair(p, carry):
        qk(2 * p + 1, 1)
        softmax_pv(2 * p, 0, None)
        qk(2 * p + 2, 0)
        softmax_pv(2 * p + 1, 1, None)
        return carry

    lax.fori_loop(0, jnp.maximum(qi - 1, 0), far_pair, 0)

    @pl.when(qi > 0)
    def _():
        qk(2 * qi - 1, 1)
        softmax_pv(2 * qi - 2, 0, None)
        qk(2 * qi, 0)
        softmax_pv(2 * qi - 1, 1, bias_ref[0, 0])

    qk(2 * qi + 1, 1, late_blocks)
    softmax_pv(2 * qi, 0, bias_ref[0, 1])
    softmax_pv(2 * qi + 1, 1, bias_ref[0, 2], late_blocks)
    acc = acc_scr[...]
    l = acc[LANES:LANES + 1]
    o = acc[:LANES, :tq] / l[:, :tq] - lam_ref[0] * (acc[:LANES, tq:] / l[:, tq:])
    ms = jnp.mean(o * o, axis=0, keepdims=True)
    o = o * lax.rsqrt(ms + RMS_EPS) * (g_ref[...] * post_scale)
    o_ref[...] = o.T.astype(o_ref.dtype)


def _attn_prompt(qt, kb, vt, rel_bias, lam, g, batch, seq_len, post_scale, tq):
    tk = tq // 2
    assert tk >= REL_MAX_DIST and tk % CHUNK == 0 and seq_len % tq == 0
    nh, _, t = qt.shape
    nq = seq_len // tq
    far_bias = rel_bias.astype(F32)[_rel_bucket(jnp.int32(-(tk + 1)))]
    table = (_bias_table(rel_bias, 3 * tk, tq, tk, far_bias) * LOG2E).reshape(nh, 3, tk, tq)
    g_col = jnp.broadcast_to(g.reshape(LANES, 1), (LANES, tq))
    acc_rows = LANES + 16
    return pl.pallas_call(
        functools.partial(_flash_kernel, tq=tq, tk=tk, cb=ATTN_COLS, post_scale=post_scale),
        grid=(batch, nh, nq),
        in_specs=[
            pl.BlockSpec(memory_space=pltpu.SMEM),
            pl.BlockSpec((1, LANES, tq), lambda b, h, qi: (h, 0, b * nq + qi)),
            pl.BlockSpec((seq_len, LANES), lambda b, h, qi: (b, h)),
            pl.BlockSpec((1, LANES, seq_len), lambda b, h, qi: (h, 0, b)),
            pl.BlockSpec((1, 3, tk, tq), lambda b, h, qi: (h, 0, 0, 0)),
            pl.BlockSpec((LANES, tq), lambda b, h, qi: (0, 0)),
        ],
        out_specs=pl.BlockSpec((tq, LANES), lambda b, h, qi: (b * nq + qi, h)),
        out_shape=jax.ShapeDtypeStruct((t, nh * LANES), BF16),
        scratch_shapes=[
            pltpu.VMEM((LANES, 2 * tq), BF16),
            pltpu.VMEM((tk, 2 * tq), F32),
            pltpu.VMEM((tk, 2 * tq), F32),
            pltpu.VMEM((1, 2 * tq), F32),
            pltpu.VMEM((acc_rows, 2 * tq), F32),
        ],
        compiler_params=_cparams(("arbitrary", "arbitrary", "arbitrary")),
    )(lam, qt, kb, vt, table, g_col)


def _sample_attn_kernel(lam_ref, q_ref, ck_ref, cv_ref, kn_ref, vn_ref, bias_ref, g_ref, o_ref,
                        *, past, post_scale):
    tq = q_ref.shape[0]
    lhs = _split_heads_lhs(q_ref[...])
    bias = bias_ref[0]
    bias2 = jnp.concatenate([bias, bias], axis=0)
    s_c = _nt_dot(lhs, ck_ref[0].astype(BF16)) + bias2[:, :past]
    s_n = _nt_dot(lhs, kn_ref[...]) + bias2[:, past:]
    m = jnp.maximum(jnp.max(s_c, axis=1, keepdims=True), jnp.max(s_n, axis=1, keepdims=True))
    p_c = jnp.exp(s_c - m)
    p_n = jnp.exp(s_n - m)
    l = jnp.sum(p_c, axis=1, keepdims=True) + jnp.sum(p_n, axis=1, keepdims=True)
    acc = (jnp.dot(p_c.astype(BF16), cv_ref[0].astype(BF16), preferred_element_type=F32)
           + jnp.dot(p_n.astype(BF16), vn_ref[...], preferred_element_type=F32))
    o = _diff_finish(acc, l, lam_ref[0], g_ref[...], tq, post_scale)
    o_ref[...] = o.astype(o_ref.dtype)


def _attn_sample(qb, kb, vb, cache_k, cache_v, rel_bias, lam, g_row, post_scale):
    nstream, past, width = cache_k.shape
    nh = width // LANES
    ls = qb.shape[0] // nstream
    no_shift = jnp.zeros((nh,), F32)
    table = jnp.swapaxes(_bias_table(rel_bias, past + ls, ls, past, no_shift), 1, 2)
    new = lambda b, h: (b, h)
    old = lambda b, h: (b, 0, h)
    return pl.pallas_call(
        functools.partial(_sample_attn_kernel, past=past, post_scale=post_scale),
        grid=(nstream, nh),
        in_specs=[
            pl.BlockSpec(memory_space=pltpu.SMEM),
            pl.BlockSpec((ls, LANES), new),
            pl.BlockSpec((1, past, LANES), old),
            pl.BlockSpec((1, past, LANES), old),
            pl.BlockSpec((ls, LANES), new),
            pl.BlockSpec((ls, LANES), new),
            pl.BlockSpec((1, ls, past + ls), lambda b, h: (h, 0, 0)),
            pl.BlockSpec((1, LANES), lambda b, h: (0, 0)),
        ],
        out_specs=pl.BlockSpec((ls, LANES), new),
        out_shape=jax.ShapeDtypeStruct(qb.shape, BF16),
        compiler_params=_cparams(("arbitrary", "arbitrary")),
    )(lam, qb, cache_k, cache_v, kb, vb, table, g_row)


def _mix_kernel(x_ref, y_ref, a_ref, wglu_ref, bglu_ref, wout_ref, g1_ref, b1_ref,
                wrh_ref, wrl_ref, br_ref, h_ref, lg_ref):
    half = y_ref.shape[1]
    g = jax.nn.gelu(y_ref[...])
    t = jnp.dot(g.astype(BF16), wglu_ref[...], preferred_element_type=F32) + bglu_ref[...]
    so = g * jax.nn.sigmoid(t)
    mix = (jnp.dot(so.astype(BF16), wout_ref[0:half, :], preferred_element_type=F32)
           + jnp.dot(a_ref[...], wout_ref[half:2 * half, :], preferred_element_type=F32))
    h = _layer_norm(DN_ALPHA * x_ref[...] + mix, g1_ref[...], b1_ref[...])
    h_ref[...] = h
    h_hi = h.astype(BF16)
    h_lo = (h - h_hi.astype(F32)).astype(BF16)
    lg = (jnp.dot(h_hi, wrh_ref[...], preferred_element_type=F32)
          + jnp.dot(h_lo, wrh_ref[...], preferred_element_type=F32)
          + jnp.dot(h_hi, wrl_ref[...], preferred_element_type=F32))
    lg_ref[...] = lg + br_ref[...]


def _mix(x2d, y_ssm, attn, wglu_b, bglu, wout_b, g1, b1, wr_hi, wr_lo, br, tm):
    t, d = x2d.shape
    half = d // 2
    row = lambda i: (i, 0)
    fix = lambda i: (0, 0)
    once = pl.Buffered(1)
    return pl.pallas_call(
        _mix_kernel,
        grid=(t // tm,),
        in_specs=[
            pl.BlockSpec((tm, d), row),
            pl.BlockSpec((tm, half), row),
            pl.BlockSpec((tm, half), row),
            pl.BlockSpec((half, half), fix, pipeline_mode=once),
            pl.BlockSpec((1, half), fix),
            pl.BlockSpec((d, d), fix, pipeline_mode=once),
            pl.BlockSpec((1, d), fix),
            pl.BlockSpec((1, d), fix),
            pl.BlockSpec((d, LANES), fix),
            pl.BlockSpec((d, LANES), fix),
            pl.BlockSpec((1, LANES), fix),
        ],
        out_specs=[pl.BlockSpec((tm, d), row), pl.BlockSpec((tm, LANES), row)],
        out_shape=[jax.ShapeDtypeStruct((t, d), F32), jax.ShapeDtypeStruct((t, LANES), F32)],
        compiler_params=_cparams(("arbitrary",)),
    )(x2d, y_ssm, attn, wglu_b, bglu, wout_b, g1, b1, wr_hi, wr_lo, br)


def _route_kernel(lg_ref, dest_ref, gate_ref, cnt_ref, tot_scr, run_scr, start_scr):
    ph = pl.program_id(0)
    i = pl.program_id(1)
    tm = lg_ref.shape[0]
    ng = N_EXPERT_GROUPS

    @pl.when((ph == 0) & (i == 0))
    def _():
        tot_scr[...] = jnp.zeros_like(tot_scr)

    lg = lg_ref[...]
    lane = lax.broadcasted_iota(jnp.int32, lg.shape, 1)
    is_g = lane < ng
    l1 = jnp.where(is_g, lg, -jnp.inf)
    m1 = jnp.max(l1, axis=1, keepdims=True)
    z1 = jnp.sum(jnp.where(is_g, jnp.exp(l1 - m1), 0.0), axis=1, keepdims=True)
    w_grp = 1.0 / z1
    el = (lane - ng).astype(F32)
    none = float(LANES)
    grp = jnp.min(jnp.where(l1 == m1, lane.astype(F32), none), axis=1, keepdims=True)
    lo = grp * EXPERTS_PER_GROUP
    in_grp = jnp.where(el >= lo, jnp.where(el < lo + EXPERTS_PER_GROUP, 1.0, 0.0), 0.0) > 0.5
    l2 = jnp.where(in_grp, lg, -jnp.inf)
    t1 = jnp.max(l2, axis=1, keepdims=True)
    i1 = jnp.min(jnp.where(l2 == t1, el, none), axis=1, keepdims=True)
    l2b = jnp.where(el == i1, -jnp.inf, l2)
    t2 = jnp.max(l2b, axis=1, keepdims=True)
    i2 = jnp.min(jnp.where(l2b == t2, el, none), axis=1, keepdims=True)
    e2 = jnp.exp(t2 - t1)
    den = 1.0 + e2
    g0 = w_grp * (1.0 / den)
    g1 = w_grp * (e2 / den)
    hit1 = el == i1
    hit2 = el == i2
    oh = jnp.where(hit1, 1.0, jnp.where(hit2, 1.0, 0.0))

    @pl.when(ph == 0)
    def _():
        tot_scr[...] = tot_scr[...] + jnp.sum(oh, axis=0, keepdims=True)

    @pl.when(ph == 1)
    def _():
        @pl.when(i == 0)
        def _():
            cnt = tot_scr[...]
            padded = jnp.floor((cnt + (MOE_ROWS - 1)) * (1.0 / MOE_ROWS)) * MOE_ROWS
            padded = jnp.broadcast_to(padded, (8, LANES))
            lane8 = lax.broadcasted_iota(jnp.int32, padded.shape, 1)
            ends = padded
            for sh in (1, 2, 4, 8, 16, 32):
                ends = ends + jnp.where(lane8 >= sh, pltpu.roll(ends, sh, axis=1), 0.0)
            start_scr[...] = (ends - padded)[0:1]
            run_scr[...] = jnp.zeros_like(run_scr)
            cnt_ref[...] = cnt

        r_id = lax.broadcasted_iota(jnp.int32, (tm, tm), 0)
        c_id = lax.broadcasted_iota(jnp.int32, (tm, tm), 1)
        tri = jnp.where(c_id < r_id, 1.0, 0.0).astype(BF16)
        place = (jnp.dot(tri, oh.astype(BF16), preferred_element_type=F32)
                 + (run_scr[...] + start_scr[...]))
        d1 = jnp.sum(jnp.where(hit1, place, 0.0), axis=1, keepdims=True)
        d2 = jnp.sum(jnp.where(hit2, place, 0.0), axis=1, keepdims=True)
        run_scr[...] = run_scr[...] + jnp.sum(oh, axis=0, keepdims=True)
        dmat = jnp.where(lane == 0, d1, jnp.where(lane == 1, d2, 0.0))
        dest_ref[0] = dmat.T[0:8, :].astype(jnp.int32)
        gate_ref[...] = jnp.where(lane == 0, g0, jnp.where(lane == 1, g1, 0.0))


def _route(logits, tm):
    t = logits.shape[0]
    placed = lambda ph, i: (ph * i, 0)
    return pl.pallas_call(
        _route_kernel,
        grid=(2, t // tm),
        in_specs=[pl.BlockSpec((tm, LANES), lambda ph, i: (i, 0))],
        out_specs=[
            pl.BlockSpec((1, 8, tm), lambda ph, i: (ph * i, 0, 0)),
            pl.BlockSpec((tm, LANES), placed),
            pl.BlockSpec((1, LANES), lambda ph, i: (0, 0)),
        ],
        out_shape=[
            jax.ShapeDtypeStruct((t // tm, 8, tm), jnp.int32),
            jax.ShapeDtypeStruct((t, LANES), F32),
            jax.ShapeDtypeStruct((1, LANES), F32),
        ],
        scratch_shapes=[pltpu.VMEM((1, LANES), F32)] * 3,
        compiler_params=_cparams(("arbitrary", "arbitrary")),
    )(logits)


def _row_copy(src, s_row, dst, d_row, sem):
    return pltpu.make_async_copy(src.at[pl.ds(s_row, 1), :], dst.at[pl.ds(d_row, 1), :], sem)


def _dispatch_kernel(dest_ref, h_ref, xin_ref, xbuf_ref, sem):
    del xin_ref
    tm = h_ref.shape[0]

    def issue(r, c):
        _row_copy(h_ref, r, xbuf_ref, dest_ref[0, 0, r], sem).start()
        _row_copy(h_ref, r, xbuf_ref, dest_ref[0, 1, r], sem).start()
        return c

    lax.fori_loop(0, tm, issue, 0, unroll=DMA_UNROLL)

    def drain(r, c):
        _row_copy(h_ref, 0, xbuf_ref, 0, sem).wait()
        _row_copy(h_ref, 0, xbuf_ref, 0, sem).wait()
        return c

    lax.fori_loop(0, tm, drain, 0, unroll=DMA_UNROLL)


def _dest_spec(tm_route, tm):
    per = tm_route // tm
    return pl.BlockSpec((1, 8, tm), lambda i: (i // per, 0, i % per), memory_space=pltpu.SMEM)


def _dispatch(h, dest, n_rows, tm):
    t, d = h.shape
    xbuf0 = jnp.zeros((n_rows, d), h.dtype)
    return pl.pallas_call(
        _dispatch_kernel,
        grid=(t // tm,),
        in_specs=[
            _dest_spec(dest.shape[2], tm),
            pl.BlockSpec((tm, d), lambda i: (i, 0)),
            pl.BlockSpec(memory_space=pl.ANY),
        ],
        out_specs=pl.BlockSpec(memory_space=pl.ANY),
        out_shape=jax.ShapeDtypeStruct((n_rows, d), h.dtype),
        scratch_shapes=[pltpu.SemaphoreType.DMA(())],
        input_output_aliases={2: 0},
        compiler_params=_cparams(("arbitrary",)),
    )(dest, h, xbuf0)


def _expert_kernel(be_ref, nused_ref, x_ref, wg_ref, wu_ref, wd_ref, y_ref):
    i = pl.program_id(0)

    @pl.when(i < nused_ref[0])
    def _():
        x = x_ref[...].astype(BF16)
        gt = jnp.dot(x, wg_ref[0], preferred_element_type=F32)
        up = jnp.dot(x, wu_ref[0], preferred_element_type=F32)
        a = (jax.nn.silu(gt) * up).astype(BF16)
        y_ref[...] = jnp.dot(a, wd_ref[0], preferred_element_type=F32)

    @pl.when(i >= nused_ref[0])
    def _():
        y_ref[...] = jnp.zeros_like(y_ref)


def _experts(xbuf, block_e, n_used, wg_b, wu_b, wd_b):
    n_rows, d = xbuf.shape
    de = wg_b.shape[-1]
    nb = n_rows // MOE_ROWS
    wmap = lambda i, be, nu: (be[i], 0, 0)
    grid_spec = pltpu.PrefetchScalarGridSpec(
        num_scalar_prefetch=2,
        grid=(nb,),
        in_specs=[
            pl.BlockSpec((MOE_ROWS, d), lambda i, be, nu: (i, 0)),
            pl.BlockSpec((1, d, de), wmap),
            pl.BlockSpec((1, d, de), wmap),
            pl.BlockSpec((1, de, d), wmap),
        ],
        out_specs=pl.BlockSpec((MOE_ROWS, d), lambda i, be, nu: (i, 0)),
    )
    return pl.pallas_call(
        _expert_kernel,
        grid_spec=grid_spec,
        out_shape=jax.ShapeDtypeStruct((n_rows, d), F32),
        compiler_params=_cparams(("arbitrary",)),
    )(block_e, n_used, xbuf, wg_b, wu_b, wd_b)


def _combine_kernel(dest_ref, h_ref, gate_ref, g2_ref, b2_ref, ybuf_ref, o_ref, ya_scr, yb_scr, sem):
    tm = h_ref.shape[0]

    def issue(r, c):
        _row_copy(ybuf_ref, dest_ref[0, 0, r], ya_scr, r, sem).start()
        _row_copy(ybuf_ref, dest_ref[0, 1, r], yb_scr, r, sem).start()
        return c

    lax.fori_loop(0, tm, issue, 0, unroll=DMA_UNROLL)

    def drain(r, c):
        _row_copy(ybuf_ref, 0, ya_scr, 0, sem).wait()
        _row_copy(ybuf_ref, 0, yb_scr, 0, sem).wait()
        return c

    lax.fori_loop(0, tm, drain, 0, unroll=DMA_UNROLL)
    gate = gate_ref[...]
    f = ya_scr[...] * gate[:, 0:1] + yb_scr[...] * gate[:, 1:2]
    o_ref[...] = _layer_norm(DN_ALPHA * h_ref[...] + f, g2_ref[...], b2_ref[...])


def _combine(h, gate, dest, ybuf, g2, b2, tm):
    t, d = h.shape
    row = lambda i: (i, 0)
    fix = lambda i: (0, 0)
    return pl.pallas_call(
        _combine_kernel,
        grid=(t // tm,),
        in_specs=[
            _dest_spec(dest.shape[2], tm),
            pl.BlockSpec((tm, d), row),
            pl.BlockSpec((tm, LANES), row),
            pl.BlockSpec((1, d), fix),
            pl.BlockSpec((1, d), fix),
            pl.BlockSpec(memory_space=pl.ANY),
        ],
        out_specs=pl.BlockSpec((tm, d), row),
        out_shape=jax.ShapeDtypeStruct((t, d), F32),
        scratch_shapes=[pltpu.VMEM((tm, d), F32), pltpu.VMEM((tm, d), F32), pltpu.SemaphoreType.DMA(())],
        compiler_params=_cparams(("arbitrary",)),
    )(dest, h, gate, g2, b2, ybuf)


def _moe_finish(h, logits, wg_b, wu_b, wd_b, g2, b2, tm_route, tm_rows):
    t = h.shape[0]
    dest, gate, cnt = _route(logits, tm_route)
    counts = cnt[0, N_EXPERT_GROUPS:N_EXPERT_GROUPS + N_EXPERTS].astype(jnp.int32)
    pad_end = jnp.cumsum((counts + MOE_ROWS - 1) // MOE_ROWS * MOE_ROWS)
    n_blocks = -(-(2 * t) // MOE_ROWS) + N_EXPERTS
    first_row = jnp.arange(n_blocks, dtype=jnp.int32)[:, None] * MOE_ROWS
    block_e = jnp.minimum(jnp.sum((pad_end[None, :] <= first_row).astype(jnp.int32), axis=1),
                          N_EXPERTS - 1)
    n_used = (pad_end[-1:] // MOE_ROWS).astype(jnp.int32)
    xbuf = _dispatch(h, dest, n_blocks * MOE_ROWS, tm_rows)
    ybuf = _experts(xbuf, block_e, n_used, wg_b, wu_b, wd_b)
    return _combine(h, gate, dest, ybuf, g2, b2, tm_rows)


def _pick(t, pref):
    tm = min(pref, t)
    while t % tm:
        tm //= 2
    return tm


def _stream(x, h0_re, h0_im, cache_k, cache_v, params, lam, lam_init):
    (w_in_b, tables, d_row, wglu_b, bglu, rel_bias, g_row, wout_b, g1, b1,
     wr_hi, wr_lo, br, wg_b, wu_b, wd_b, g2, b2) = params
    b, l, d = x.shape
    t = b * l
    x2d = x.reshape(t, d)
    u, k, v, qb, kb, vb, qt, vt = _in_proj(x2d, w_in_b, _pick(t, 512))
    y_ssm, h_re, h_im = _ssm(u, tables, d_row, h0_re, h0_im, l, _pick(t, 4096))
    post = 1.0 - lam_init
    if cache_k is None:
        attn = _attn_prompt(qt, kb, vt, rel_bias, lam, g_row, b, l, post, _pick(l, ATTN_TILE))
    else:
        attn = _attn_sample(qb, kb, vb, cache_k, cache_v, rel_bias, lam, g_row, post)
    h, logits = _mix(x2d, y_ssm, attn, wglu_b, bglu, wout_b, g1, b1, wr_hi, wr_lo, br, _pick(t, 512))
    out = _moe_finish(h, logits, wg_b, wu_b, wd_b, g2, b2, _pick(t, 512), _pick(t, 256))
    return out.reshape(b, l, d), k, v, h_re, h_im


def kernel(x_prompt, x_sample, cache_k, cache_v, state_ssm_re, state_ssm_im, w_in, ssm_a_re, ssm_a_im, ssm_log_dt, ssm_b_re, ssm_b_im, ssm_c_re, ssm_c_im, ssm_d, w_glu, b_glu, lambda_q1, lambda_k1, lambda_q2, lambda_k2, subln_g, rel_bias, w_out, ln1_g, ln1_b, w_r1, b_r1, w_r2, b_r2, w_gate, w_up, w_down, ln2_g, ln2_b):
    assert w_in.shape[0] == DEPTH
    bp, lp, d = x_prompt.shape
    bs, ls, _ = x_sample.shape
    past = cache_k.shape[2]
    nh, dqk = cache_k.shape[3], cache_k.shape[4]
    n_grp, n_state = state_ssm_re.shape[2], state_ssm_re.shape[3]
    l = 0
    lam_init = 0.8 - 0.6 * math.exp(-0.3 * l)
    lam = (jnp.exp(jnp.sum(lambda_q1[l].astype(F32) * lambda_k1[l].astype(F32)))
           - jnp.exp(jnp.sum(lambda_q2[l].astype(F32) * lambda_k2[l].astype(F32))) + lam_init).reshape(1)
    tables = _ssm_tables(ssm_a_re[l].astype(F32), ssm_a_im[l].astype(F32), ssm_log_dt[l].astype(F32),
                         ssm_b_re[l].astype(F32), ssm_b_im[l].astype(F32),
                         ssm_c_re[l].astype(F32), ssm_c_im[l].astype(F32))
    wr = jnp.concatenate([w_r1[l].astype(F32), w_r2[l].astype(F32).reshape(d, N_EXPERTS)], axis=1)
    wr = jnp.pad(wr, ((0, 0), (0, LANES - wr.shape[1])))
    wr_hi = wr.astype(BF16)
    wr_lo = (wr - wr_hi.astype(F32)).astype(BF16)
    br = jnp.concatenate([b_r1[l].astype(F32), b_r2[l].astype(F32).reshape(N_EXPERTS)])
    br = jnp.pad(br, (0, LANES - br.shape[0])).reshape(1, LANES)
    params = (
        w_in[l].astype(BF16), tables, ssm_d[l].astype(F32).reshape(1, -1),
        w_glu[l].astype(BF16), b_glu[l].astype(F32).reshape(1, -1), rel_bias,
        subln_g[l].astype(F32).reshape(1, -1), w_out[l].astype(BF16),
        ln1_g[l].astype(F32).reshape(1, -1), ln1_b[l].astype(F32).reshape(1, -1),
        wr_hi, wr_lo, br,
        w_gate[l].astype(BF16), w_up[l].astype(BF16), w_down[l].astype(BF16),
        ln2_g[l].astype(F32).reshape(1, -1), ln2_b[l].astype(F32).reshape(1, -1),
    )
    zeros = jnp.zeros((bp, n_grp * n_state), F32)
    yp, kp, vp, hp_re, hp_im = _stream(x_prompt, zeros, zeros, None, None, params, lam, lam_init)
    ys, ks, vs, hs_re, hs_im = _stream(
        x_sample, state_ssm_re[l].astype(F32).reshape(bs, -1), state_ssm_im[l].astype(F32).reshape(bs, -1),
        cache_k[l].reshape(bs, past, nh * dqk), cache_v[l].reshape(bs, past, -1), params, lam, lam_init)
    return (yp, ys,
            kp.reshape(1, bp, lp, nh, dqk).astype(cache_k.dtype),
            vp.reshape(1, bp, lp, nh, -1).astype(cache_v.dtype),
            hp_re.reshape(1, bp, n_grp, n_state).astype(state_ssm_re.dtype),
            hp_im.reshape(1, bp, n_grp, n_state).astype(state_ssm_im.dtype),
            ks.reshape(1, bs, ls, nh, dqk).astype(cache_k.dtype),
            vs.reshape(1, bs, ls, nh, -1).astype(cache_v.dtype),
            hs_re.reshape(1, bs, n_grp, n_state).astype(state_ssm_re.dtype),
            hs_im.reshape(1, bs, n_grp, n_state).astype(state_ssm_im.dtype))
```

```python
import functools
import math

import jax
import jax.numpy as jnp
from jax import lax
from jax.experimental import pallas as pl
from jax.experimental.pallas import tpu as pltpu

F32 = jnp.float32
BF16 = jnp.bfloat16

CHUNK = 64
SSM_GROUP_CH = 16
SSM_STATE = 64
N_ATTN_HEADS = 8
ATTN_HEAD_DIM = 64
REL_BUCKETS = 32
REL_MAX_DIST = 128
N_EXPERT_GROUPS = 4
EXPERTS_PER_GROUP = 8
N_EXPERTS = N_EXPERT_GROUPS * EXPERTS_PER_GROUP
DEPTH = 1
DN_ALPHA = (2 * DEPTH) ** 0.25
LN_EPS = 1e-5
RMS_EPS = 1e-5
NEG_INF = -1e30
LOG2E = math.log2(math.e)

LANES = 128
VMEM_LIMIT = 56 * 1024 * 1024

SSM_SUB = 16
SSM_OCT = LANES // SSM_GROUP_CH
ATTN_TILE = 1024
ATTN_COLS = 512
MOE_ROWS = 256
DMA_UNROLL = 8


def _cparams(sem, vmem=VMEM_LIMIT):
    return pltpu.CompilerParams(dimension_semantics=sem, vmem_limit_bytes=vmem)


def _layer_norm(r, g, b):
    mu = jnp.mean(r, axis=-1, keepdims=True)
    rc = r - mu
    var = jnp.mean(rc * rc, axis=-1, keepdims=True)
    return rc * lax.rsqrt(var + LN_EPS) * g + b


def _store_heads_transposed(dst_ref, z):
    for h in range(dst_ref.shape[0]):
        dst_ref[h] = z[:, LANES * h:LANES * (h + 1)].T.astype(BF16)


def _store_heads_split(dst_ref, z):
    for h in range(dst_ref.shape[1]):
        dst_ref[:, h, :] = z[:, LANES * h:LANES * (h + 1)]


def _in_proj_kernel(x_ref, w_ref, u_ref, k_ref, v_ref, qb_ref, kb_ref, vb_ref, qt_ref, vt_ref, xb_scr,
                    *, q_scale):
    j = pl.program_id(1)

    @pl.when(j == 0)
    def _():
        xb_scr[...] = x_ref[...].astype(BF16)

    n = u_ref.shape[1]
    w = w_ref[:, pl.ds(pl.multiple_of(j * n, n), n)]
    z = jnp.dot(xb_scr[...], w, preferred_element_type=F32)

    @pl.when(j == 0)
    def _():
        u_ref[...] = z

    @pl.when(j == 1)
    def _():
        zq = z * q_scale
        qb_ref[...] = zq.astype(BF16)
        _store_heads_transposed(qt_ref, zq * LOG2E)

    @pl.when(j == 2)
    def _():
        _store_heads_split(k_ref, z)
        kb_ref[...] = z.astype(BF16)

    @pl.when(j == 3)
    def _():
        _store_heads_split(v_ref, z)
        vb_ref[...] = z.astype(BF16)
        _store_heads_transposed(vt_ref, z)


def _in_proj(x2d, w_in_b, tm):
    t, d = x2d.shape
    n = w_in_b.shape[1] // 4
    nh = n // LANES
    row = lambda i, j: (i, 0)
    f32_out = jax.ShapeDtypeStruct((t, n), F32)
    b16_out = jax.ShapeDtypeStruct((t, n), BF16)
    tr_out = jax.ShapeDtypeStruct((nh, LANES, t), BF16)
    tr_spec = pl.BlockSpec((nh, LANES, tm), lambda i, j: (0, 0, i))
    cache_out = jax.ShapeDtypeStruct((t, nh, LANES), F32)
    cache_spec = pl.BlockSpec((tm, nh, LANES), lambda i, j: (i, 0, 0))
    return pl.pallas_call(
        functools.partial(_in_proj_kernel, q_scale=ATTN_HEAD_DIM ** -0.5),
        grid=(t // tm, 4),
        in_specs=[pl.BlockSpec((tm, d), row),
                  pl.BlockSpec((d, 4 * n), lambda i, j: (0, 0), pipeline_mode=pl.Buffered(1))],
        out_specs=[pl.BlockSpec((tm, n), row), cache_spec, cache_spec] + [pl.BlockSpec((tm, n), row)] * 3
        + [tr_spec] * 2,
        out_shape=[f32_out, cache_out, cache_out, b16_out, b16_out, b16_out, tr_out, tr_out],
        scratch_shapes=[pltpu.VMEM((tm, d), BF16)],
        compiler_params=_cparams(("arbitrary", "arbitrary")),
    )(x2d, w_in_b)


def _ssm_tables(a_re, a_im, log_dt, b_re, b_im, c_re, c_im):
    hp = lax.Precision.HIGHEST
    g, p = a_re.shape
    c = b_re.shape[-1]
    s, o = SSM_SUB, SSM_OCT
    dt = jnp.exp(log_dt)[:, None]
    mag = jnp.exp(a_re * dt)
    ab_re = mag * jnp.cos(a_im * dt)
    ab_im = mag * jnp.sin(a_im * dt)
    nr = ab_re - 1.0
    ni = ab_im
    den = a_re * a_re + a_im * a_im
    z_re = ((nr * a_re + ni * a_im) / den)[..., None]
    z_im = ((ni * a_re - nr * a_im) / den)[..., None]
    bb_re = z_re * b_re - z_im * b_im
    bb_im = z_re * b_im + z_im * b_re
    n = jnp.arange(s + 1, dtype=F32)[:, None, None]
    pmag = jnp.exp(n * (a_re * dt))
    pw_re = pmag * jnp.cos(n * (a_im * dt))
    pw_im = pmag * jnp.sin(n * (a_im * dt))
    ca_re = c_re[None] * pw_re[:, :, None, :] - c_im[None] * pw_im[:, :, None, :]
    ca_im = c_re[None] * pw_im[:, :, None, :] + c_im[None] * pw_re[:, :, None, :]
    kern = (jnp.einsum('dgop,gpi->dgio', ca_re[:s], bb_re, precision=hp)
            - jnp.einsum('dgop,gpi->dgio', ca_im[:s], bb_im, precision=hp))
    n_oct = g // o

    def expand_groups(tbl, col_group):
        j, n = tbl.shape[1], tbl.shape[3]
        full = jnp.broadcast_to(tbl[:, :, None], (n_oct, j, o, c, n)).reshape(n_oct, j * o * c, n)
        row_l = (lax.broadcasted_iota(jnp.int32, (j * o * c, n), 0) // c) % o
        col_l = col_group(lax.broadcasted_iota(jnp.int32, (j * o * c, n), 1))
        return jnp.where((row_l == col_l)[None], full, 0.0).astype(BF16)

    kt = jnp.transpose(kern.reshape(s, n_oct, o, c, c), (1, 3, 0, 2, 4))
    lags = expand_groups(kt.reshape(n_oct, 1, c, s * o * c), lambda col: (col % LANES) // c)
    m = jnp.concatenate(
        [jnp.pad(lags[:, :, :LANES * (s - j)], ((0, 0), (0, 0), (LANES * j, 0))) for j in range(s)],
        axis=1)
    nrev = (s - 1) - jnp.arange(s, dtype=F32)[:, None, None]
    rmag = jnp.exp(nrev * (a_re * dt))
    rv_re = rmag * jnp.cos(nrev * (a_im * dt))
    rv_im = rmag * jnp.sin(nrev * (a_im * dt))
    wb_re = rv_re[..., None] * bb_re[None] - rv_im[..., None] * bb_im[None]
    wb_im = rv_re[..., None] * bb_im[None] + rv_im[..., None] * bb_re[None]
    state_group = lambda col: (col % (o * p)) // p
    wb = jnp.stack([wb_re, wb_im], axis=0).reshape(2, s, n_oct, o, p, c)
    wt = jnp.transpose(wb, (2, 1, 5, 0, 3, 4)).reshape(n_oct, s, c, 2 * o * p)
    w = expand_groups(wt, state_group)
    zc = jnp.stack([ca_re[1:], -ca_im[1:]], axis=0).reshape(2, s, n_oct, o, c, p)
    zt = jnp.transpose(zc, (2, 1, 4, 0, 3, 5)).reshape(n_oct, s, c, 2 * o * p)
    z_t = expand_groups(zt, state_group)
    return m, w, z_t, a_re * dt, a_im * dt


def _state_powers(exp_re, exp_im, n_pow, n_oct):
    n = (SSM_SUB * jnp.arange(n_pow, dtype=F32))[:, None, None]
    mag = jnp.exp(n * exp_re)
    pw_re = (mag * jnp.cos(n * exp_im)).reshape(n_pow, n_oct, -1)
    pw_im = (mag * jnp.sin(n * exp_im)).reshape(n_pow, n_oct, -1)
    return jnp.swapaxes(pw_re, 0, 1), jnp.swapaxes(pw_im, 0, 1)


def _ssm_kernel(u_ref, m_ref, w_ref, zt_ref, pre_ref, pim_ref, d_ref, h0re_ref, h0im_ref,
                y_ref, hre_ref, him_ref, lhs_scr, v_scr, sin_scr, cre_scr, cim_scr, *, rows, cps):
    i = pl.program_id(1)
    half = cre_scr.shape[1]

    @pl.when(i == 0)
    def _():
        cre_scr[...] = jnp.zeros_like(cre_scr)
        cim_scr[...] = jnp.zeros_like(cim_scr)

    for j in range(SSM_SUB):
        piece = u_ref[pl.ds(j, rows, stride=SSM_SUB), :]
        lhs_scr[:, LANES * j:LANES * (j + 1)] = piece.astype(BF16)
    lhs = lhs_scr[...]
    nt = half // LANES
    v = jnp.dot(lhs, w_ref[0], preferred_element_type=F32)
    for c in range(2 * nt):
        v_scr[c] = v[:, LANES * c:LANES * (c + 1)]
    seg = rows // 8
    a_re = pre_ref[0, 1:2, :]
    a_im = pim_ref[0, 1:2, :]

    def tiles(ref, r, part):
        return jnp.concatenate([ref[part * nt + c, r, :] for c in range(nt)], axis=1)

    def put_tiles(ref, r, part, val):
        for c in range(nt):
            ref[part * nt + c, r, :] = val[:, LANES * c:LANES * (c + 1)]

    def local(t, carry):
        l_re, l_im = carry
        r = pl.ds(t, 8, stride=seg)
        put_tiles(sin_scr, r, 0, l_re)
        put_tiles(sin_scr, r, 1, l_im)
        v_re = tiles(v_scr, r, 0)
        v_im = tiles(v_scr, r, 1)
        return a_re * l_re - a_im * l_im + v_re, a_re * l_im + a_im * l_re + v_im

    zero = jnp.zeros((8, half), F32)
    e_re, e_im = lax.fori_loop(0, seg, local, (zero, zero))

    aseg_re = pre_ref[0, seg:seg + 1, :]
    aseg_im = pim_ref[0, seg:seg + 1, :]
    c_re = cre_scr[...]
    c_im = cim_scr[...]
    ins_re, ins_im = [], []
    for s in range(8):
        first = i * rows + s * seg
        seq = first // cps
        start = (first % cps) == 0
        c_re = jnp.where(start, h0re_ref[pl.ds(seq, 1), :], c_re)
        c_im = jnp.where(start, h0im_ref[pl.ds(seq, 1), :], c_im)
        ins_re.append(c_re)
        ins_im.append(c_im)
        n_re = aseg_re * c_re - aseg_im * c_im + e_re[s:s + 1]
        n_im = aseg_re * c_im + aseg_im * c_re + e_im[s:s + 1]
        hre_ref[pl.ds(seq, 1), :] = n_re
        him_ref[pl.ds(seq, 1), :] = n_im
        c_re, c_im = n_re, n_im
    cre_scr[...] = c_re
    cim_scr[...] = c_im
    in_re = jnp.concatenate(ins_re, axis=0)
    in_im = jnp.concatenate(ins_im, axis=0)

    def carry_in(t, c):
        r = pl.ds(t, 8, stride=seg)
        p_re = pre_ref[0, pl.ds(t, 1), :]
        p_im = pim_ref[0, pl.ds(t, 1), :]
        put_tiles(sin_scr, r, 0, tiles(sin_scr, r, 0) + (p_re * in_re - p_im * in_im))
        put_tiles(sin_scr, r, 1, tiles(sin_scr, r, 1) + (p_re * in_im + p_im * in_re))
        return c

    lax.fori_loop(0, seg, carry_in, 0)

    s_in = jnp.concatenate([sin_scr[c] for c in range(2 * nt)], axis=1).astype(BF16)
    y = (jnp.dot(lhs, m_ref[0], preferred_element_type=F32)
         + lax.dot_general(s_in, zt_ref[0], (((1,), (1,)), ((), ())), preferred_element_type=F32))
    d = d_ref[...]
    for j in range(SSM_SUB):
        uj = u_ref[pl.ds(j, rows, stride=SSM_SUB), :]
        y_ref[pl.ds(j, rows, stride=SSM_SUB), :] = y[:, LANES * j:LANES * (j + 1)] + d * uj


def _ssm(u, tables, d_row, h0_re, h0_im, seq_len, tm):
    m, w, z, exp_re, exp_im = tables
    t, width = u.shape
    n_oct = width // LANES
    nseq = h0_re.shape[0]
    rows = tm // SSM_SUB
    cps = seq_len // SSM_SUB
    seg = rows // 8
    assert rows % 8 == 0 and cps % seg == 0
    pw_re, pw_im = _state_powers(exp_re, exp_im, seg + 1, n_oct)
    half = pw_re.shape[-1]
    kdim = SSM_SUB * LANES
    once = pl.Buffered(1)
    col = lambda g, i: (i, g)
    per_oct3 = lambda g, i: (g, 0, 0)
    per_oct2 = lambda g, i: (0, g)
    return pl.pallas_call(
        functools.partial(_ssm_kernel, rows=rows, cps=cps),
        grid=(n_oct, t // tm),
        in_specs=[
            pl.BlockSpec((tm, LANES), col),
            pl.BlockSpec((1, kdim, kdim), per_oct3, pipeline_mode=once),
            pl.BlockSpec((1, kdim, 2 * half), per_oct3, pipeline_mode=once),
            pl.BlockSpec((1, kdim, 2 * half), per_oct3, pipeline_mode=once),
            pl.BlockSpec((1, seg + 1, half), per_oct3),
            pl.BlockSpec((1, seg + 1, half), per_oct3),
            pl.BlockSpec((1, LANES), per_oct2),
            pl.BlockSpec((nseq, half), per_oct2),
            pl.BlockSpec((nseq, half), per_oct2),
        ],
        out_specs=[
            pl.BlockSpec((tm, LANES), col),
            pl.BlockSpec((nseq, half), per_oct2),
            pl.BlockSpec((nseq, half), per_oct2),
        ],
        out_shape=[
            jax.ShapeDtypeStruct((t, width), F32),
            jax.ShapeDtypeStruct((nseq, n_oct * half), F32),
            jax.ShapeDtypeStruct((nseq, n_oct * half), F32),
        ],
        scratch_shapes=[
            pltpu.VMEM((rows, kdim), BF16),
            pltpu.VMEM((2 * half // LANES, rows, LANES), F32),
            pltpu.VMEM((2 * half // LANES, rows, LANES), F32),
            pltpu.VMEM((1, half), F32),
            pltpu.VMEM((1, half), F32),
        ],
        compiler_params=_cparams(("arbitrary", "arbitrary")),
    )(u, m, w, z, pw_re, pw_im, d_row, h0_re, h0_im)


def _rel_bucket(rel):
    nb = REL_BUCKETS // 2
    max_exact = nb // 2
    n = jnp.abs(rel)
    nf = jnp.maximum(n, 1).astype(F32)
    large = max_exact + (jnp.log(nf / max_exact) / math.log(REL_MAX_DIST / max_exact)
                         * (nb - max_exact)).astype(jnp.int32)
    large = jnp.minimum(large, nb - 1)
    return jnp.where(rel > 0, nb, 0) + jnp.where(n < max_exact, n, large)


def _bias_table(rel_bias, n_k, n_q, q0, shift):
    k_pos = lax.broadcasted_iota(jnp.int32, (n_k, n_q), 0)
    q_pos = q0 + lax.broadcasted_iota(jnp.int32, (n_k, n_q), 1)
    bucket = _rel_bucket(k_pos - q_pos)[None]
    rows = (rel_bias.astype(F32) - shift[None, :])[:, :, None, None]
    bias = jnp.broadcast_to(rows[0], (rel_bias.shape[1], n_k, n_q))
    for b in range(1, REL_BUCKETS):
        bias = jnp.where(bucket == b, rows[b], bias)
    return jnp.where((k_pos // CHUNK <= q_pos // CHUNK)[None], bias, NEG_INF)


def _split_heads_lhs(q):
    lane = lax.broadcasted_iota(jnp.int32, q.shape, 1)
    zero = jnp.zeros_like(q)
    return jnp.concatenate([jnp.where(lane < ATTN_HEAD_DIM, q, zero),
                            jnp.where(lane >= ATTN_HEAD_DIM, q, zero)], axis=0)


def _nt_dot(a, b):
    return lax.dot_general(a, b, (((1,), (1,)), ((), ())), preferred_element_type=F32)


def _diff_finish(acc, l, lam, g, tq, post_scale):
    o = acc[:tq] / l[:tq] - lam * (acc[tq:] / l[tq:])
    ms = jnp.mean(o * o, axis=-1, keepdims=True)
    return o * lax.rsqrt(ms + RMS_EPS) * g * post_scale


def _flash_kernel(lam_ref, qt_ref, k_ref, vt_ref, bias_ref, g_ref, o_ref,
                  lhs_scr, s0_scr, s1_scr, m_scr, acc_scr, *, tq, tk, cb, post_scale):
    qi = pl.program_id(2)
    qt = qt_ref[0]
    row = lax.broadcasted_iota(jnp.int32, qt.shape, 0)
    zero = jnp.zeros_like(qt)
    lhs_scr[:, 0:tq] = jnp.where(row < ATTN_HEAD_DIM, qt, zero)
    lhs_scr[:, tq:2 * tq] = jnp.where(row >= ATTN_HEAD_DIM, qt, zero)
    m_scr[...] = jnp.full_like(m_scr, NEG_INF)
    acc_scr[...] = jnp.zeros_like(acc_scr)
    ones_rows = jnp.ones((acc_scr.shape[0] - LANES, tk), BF16)
    s_bufs = (s0_scr, s1_scr)

    all_blocks = tuple(range(2 * tq // cb))
    late_blocks = tuple(c for c in all_blocks if (c * cb) % tq >= tk)

    def qk(tile, buf, blocks=None):
        k = k_ref[pl.ds(pl.multiple_of(tile * tk, tk), tk), :]
        if blocks is None:
            s_bufs[buf][...] = jnp.dot(k, lhs_scr[...], preferred_element_type=F32)
        else:
            for c in blocks:
                cols = pl.ds(c * cb, cb)
                s_bufs[buf][:, cols] = jnp.dot(k, lhs_scr[:, cols], preferred_element_type=F32)

    def softmax_pv(tile, buf, bias, blocks=all_blocks):
        vt = vt_ref[0, :, pl.ds(pl.multiple_of(tile * tk, tk), tk)]
        vt = jnp.concatenate([vt, ones_rows], axis=0)
        for c in blocks:
            cols = pl.ds(c * cb, cb)
            s = s_bufs[buf][:, cols]
            if bias is not None:
                b0 = (c * cb) % tq
                s = s + bias[:, b0:b0 + cb]
            m_prev = m_scr[:, cols]
            m_new = jnp.maximum(m_prev, jnp.max(s, axis=0, keepdims=True))
            alpha = jnp.exp2(m_prev - m_new)
            p = jnp.exp2(s - m_new).astype(BF16)
            acc_scr[:, cols] = alpha * acc_scr[:, cols] + jnp.dot(vt, p, preferred_element_type=F32)
            m_scr[:, cols] = m_new

    qk(0, 0)

    def far_pair(p, carry):
        qk(2 * p + 1, 1)
        softmax_pv(2 * p, 0, None)
        qk(2 * p + 2, 0)
        softmax_pv(2 * p + 1, 1, None)
        return carry

    lax.fori_loop(0, jnp.maximum(qi - 1, 0), far_pair, 0)

    @pl.when(qi > 0)
    def _():
        qk(2 * qi - 1, 1)
        softmax_pv(2 * qi - 2, 0, None)
        qk(2 * qi, 0)
        softmax_pv(2 * qi - 1, 1, bias_ref[0, 0])

    qk(2 * qi + 1, 1, late_blocks)
    softmax_pv(2 * qi, 0, bias_ref[0, 1])
    softmax_pv(2 * qi + 1, 1, bias_ref[0, 2], late_blocks)
    acc = acc_scr[...]
    l = acc[LANES:LANES + 1]
    o = acc[:LANES, :tq] / l[:, :tq] - lam_ref[0] * (acc[:LANES, tq:] / l[:, tq:])
    ms = jnp.mean(o * o, axis=0, keepdims=True)
    o = o * lax.rsqrt(ms + RMS_EPS) * (g_ref[...] * post_scale)
    o_ref[...] = o.T.astype(o_ref.dtype)


def _attn_prompt(qt, kb, vt, rel_bias, lam, g, batch, seq_len, post_scale, tq):
    tk = tq // 2
    assert tk >= REL_MAX_DIST and tk % CHUNK == 0 and seq_len % tq == 0
    nh, _, t = qt.shape
    nq = seq_len // tq
    far_bias = rel_bias.astype(F32)[_rel_bucket(jnp.int32(-(tk + 1)))]
    table = (_bias_table(rel_bias, 3 * tk, tq, tk, far_bias) * LOG2E).reshape(nh, 3, tk, tq)
    g_col = jnp.broadcast_to(g.reshape(LANES, 1), (LANES, tq))
    acc_rows = LANES + 16
    return pl.pallas_call(
        functools.partial(_flash_kernel, tq=tq, tk=tk, cb=ATTN_COLS, post_scale=post_scale),
        grid=(batch, nh, nq),
        in_specs=[
            pl.BlockSpec(memory_space=pltpu.SMEM),
            pl.BlockSpec((1, LANES, tq), lambda b, h, qi: (h, 0, b * nq + qi)),
            pl.BlockSpec((seq_len, LANES), lambda b, h, qi: (b, h)),
            pl.BlockSpec((1, LANES, seq_len), lambda b, h, qi: (h, 0, b)),
            pl.BlockSpec((1, 3, tk, tq), lambda b, h, qi: (h, 0, 0, 0)),
            pl.BlockSpec((LANES, tq), lambda b, h, qi: (0, 0)),
        ],
        out_specs=pl.BlockSpec((tq, LANES), lambda b, h, qi: (b * nq + qi, h)),
        out_shape=jax.ShapeDtypeStruct((t, nh * LANES), BF16),
        scratch_shapes=[
            pltpu.VMEM((LANES, 2 * tq), BF16),
            pltpu.VMEM((tk, 2 * tq), F32),
            pltpu.VMEM((tk, 2 * tq), F32),
            pltpu.VMEM((1, 2 * tq), F32),
            pltpu.VMEM((acc_rows, 2 * tq), F32),
        ],
        compiler_params=_cparams(("arbitrary", "arbitrary", "arbitrary")),
    )(lam, qt, kb, vt, table, g_col)


def _sample_attn_kernel(lam_ref, q_ref, ck_ref, cv_ref, kn_ref, vn_ref, bias_ref, g_ref, o_ref,
                        *, past, post_scale):
    tq = q_ref.shape[0]
    lhs = _split_heads_lhs(q_ref[...])
    bias = bias_ref[0]
    bias2 = jnp.concatenate([bias, bias], axis=0)
    s_c = _nt_dot(lhs, ck_ref[0].astype(BF16)) + bias2[:, :past]
    s_n = _nt_dot(lhs, kn_ref[...]) + bias2[:, past:]
    m = jnp.maximum(jnp.max(s_c, axis=1, keepdims=True), jnp.max(s_n, axis=1, keepdims=True))
    p_c = jnp.exp(s_c - m)
    p_n = jnp.exp(s_n - m)
    l = jnp.sum(p_c, axis=1, keepdims=True) + jnp.sum(p_n, axis=1, keepdims=True)
    acc = (jnp.dot(p_c.astype(BF16), cv_ref[0].astype(BF16), preferred_element_type=F32)
           + jnp.dot(p_n.astype(BF16), vn_ref[...], preferred_element_type=F32))
    o = _diff_finish(acc, l, lam_ref[0], g_ref[...], tq, post_scale)
    o_ref[...] = o.astype(o_ref.dtype)


def _attn_sample(qb, kb, vb, cache_k, cache_v, rel_bias, lam, g_row, post_scale):
    nstream, past, width = cache_k.shape
    nh = width // LANES
    ls = qb.shape[0] // nstream
    no_shift = jnp.zeros((nh,), F32)
    table = jnp.swapaxes(_bias_table(rel_bias, past + ls, ls, past, no_shift), 1, 2)
    new = lambda b, h: (b, h)
    old = lambda b, h: (b, 0, h)
    return pl.pallas_call(
        functools.partial(_sample_attn_kernel, past=past, post_scale=post_scale),
        grid=(nstream, nh),
        in_specs=[
            pl.BlockSpec(memory_space=pltpu.SMEM),
            pl.BlockSpec((ls, LANES), new),
            pl.BlockSpec((1, past, LANES), old),
            pl.BlockSpec((1, past, LANES), old),
            pl.BlockSpec((ls, LANES), new),
            pl.BlockSpec((ls, LANES), new),
            pl.BlockSpec((1, ls, past + ls), lambda b, h: (h, 0, 0)),
            pl.BlockSpec((1, LANES), lambda b, h: (0, 0)),
        ],
        out_specs=pl.BlockSpec((ls, LANES), new),
        out_shape=jax.ShapeDtypeStruct(qb.shape, BF16),
        compiler_params=_cparams(("arbitrary", "arbitrary")),
    )(lam, qb, cache_k, cache_v, kb, vb, table, g_row)


def _mix_kernel(x_ref, y_ref, a_ref, wglu_ref, bglu_ref, wout_ref, g1_ref, b1_ref,
                wrh_ref, wrl_ref, br_ref, h_ref, lg_ref):
    half = y_ref.shape[1]
    g = jax.nn.gelu(y_ref[...])
    t = jnp.dot(g.astype(BF16), wglu_ref[...], preferred_element_type=F32) + bglu_ref[...]
    so = g * jax.nn.sigmoid(t)
    mix = (jnp.dot(so.astype(BF16), wout_ref[0:half, :], preferred_element_type=F32)
           + jnp.dot(a_ref[...], wout_ref[half:2 * half, :], preferred_element_type=F32))
    h = _layer_norm(DN_ALPHA * x_ref[...] + mix, g1_ref[...], b1_ref[...])
    h_ref[...] = h
    h_hi = h.astype(BF16)
    h_lo = (h - h_hi.astype(F32)).astype(BF16)
    lg = (jnp.dot(h_hi, wrh_ref[...], preferred_element_type=F32)
          + jnp.dot(h_lo, wrh_ref[...], preferred_element_type=F32)
          + jnp.dot(h_hi, wrl_ref[...], preferred_element_type=F32))
    lg_ref[...] = lg + br_ref[...]


def _mix(x2d, y_ssm, attn, wglu_b, bglu, wout_b, g1, b1, wr_hi, wr_lo, br, tm):
    t, d = x2d.shape
    half = d // 2
    row = lambda i: (i, 0)
    fix = lambda i: (0, 0)
    once = pl.Buffered(1)
    return pl.pallas_call(
        _mix_kernel,
        grid=(t // tm,),
        in_specs=[
            pl.BlockSpec((tm, d), row),
            pl.BlockSpec((tm, half), row),
            pl.BlockSpec((tm, half), row),
            pl.BlockSpec((half, half), fix, pipeline_mode=once),
            pl.BlockSpec((1, half), fix),
            pl.BlockSpec((d, d), fix, pipeline_mode=once),
            pl.BlockSpec((1, d), fix),
            pl.BlockSpec((1, d), fix),
            pl.BlockSpec((d, LANES), fix),
            pl.BlockSpec((d, LANES), fix),
            pl.BlockSpec((1, LANES), fix),
        ],
        out_specs=[pl.BlockSpec((tm, d), row), pl.BlockSpec((tm, LANES), row)],
        out_shape=[jax.ShapeDtypeStruct((t, d), F32), jax.ShapeDtypeStruct((t, LANES), F32)],
        compiler_params=_cparams(("arbitrary",)),
    )(x2d, y_ssm, attn, wglu_b, bglu, wout_b, g1, b1, wr_hi, wr_lo, br)


def _route_kernel(lg_ref, dest_ref, gate_ref, cnt_ref, tot_scr, run_scr, start_scr):
    ph = pl.program_id(0)
    i = pl.program_id(1)
    tm = lg_ref.shape[0]
    ng = N_EXPERT_GROUPS

    @pl.when((ph == 0) & (i == 0))
    def _():
        tot_scr[...] = jnp.zeros_like(tot_scr)

    lg = lg_ref[...]
    lane = lax.broadcasted_iota(jnp.int32, lg.shape, 1)
    is_g = lane < ng
    l1 = jnp.where(is_g, lg, -jnp.inf)
    m1 = jnp.max(l1, axis=1, keepdims=True)
    z1 = jnp.sum(jnp.where(is_g, jnp.exp(l1 - m1), 0.0), axis=1, keepdims=True)
    w_grp = 1.0 / z1
    el = (lane - ng).astype(F32)
    none = float(LANES)
    grp = jnp.min(jnp.where(l1 == m1, lane.astype(F32), none), axis=1, keepdims=True)
    lo = grp * EXPERTS_PER_GROUP
    in_grp = jnp.where(el >= lo, jnp.where(el < lo + EXPERTS_PER_GROUP, 1.0, 0.0), 0.0) > 0.5
    l2 = jnp.where(in_grp, lg, -jnp.inf)
    t1 = jnp.max(l2, axis=1, keepdims=True)
    i1 = jnp.min(jnp.where(l2 == t1, el, none), axis=1, keepdims=True)
    l2b = jnp.where(el == i1, -jnp.inf, l2)
    t2 = jnp.max(l2b, axis=1, keepdims=True)
    i2 = jnp.min(jnp.where(l2b == t2, el, none), axis=1, keepdims=True)
    e2 = jnp.exp(t2 - t1)
    den = 1.0 + e2
    g0 = w_grp * (1.0 / den)
    g1 = w_grp * (e2 / den)
    hit1 = el == i1
    hit2 = el == i2
    oh = jnp.where(hit1, 1.0, jnp.where(hit2, 1.0, 0.0))

    @pl.when(ph == 0)
    def _():
        tot_scr[...] = tot_scr[...] + jnp.sum(oh, axis=0, keepdims=True)

    @pl.when(ph == 1)
    def _():
        @pl.when(i == 0)
        def _():
            cnt = tot_scr[...]
            padded = jnp.floor((cnt + (MOE_ROWS - 1)) * (1.0 / MOE_ROWS)) * MOE_ROWS
            padded = jnp.broadcast_to(padded, (8, LANES))
            lane8 = lax.broadcasted_iota(jnp.int32, padded.shape, 1)
            ends = padded
            for sh in (1, 2, 4, 8, 16, 32):
                ends = ends + jnp.where(lane8 >= sh, pltpu.roll(ends, sh, axis=1), 0.0)
            start_scr[...] = (ends - padded)[0:1]
            run_scr[...] = jnp.zeros_like(run_scr)
            cnt_ref[...] = cnt

        r_id = lax.broadcasted_iota(jnp.int32, (tm, tm), 0)
        c_id = lax.broadcasted_iota(jnp.int32, (tm, tm), 1)
        tri = jnp.where(c_id < r_id, 1.0, 0.0).astype(BF16)
        place = (jnp.dot(tri, oh.astype(BF16), preferred_element_type=F32)
                 + (run_scr[...] + start_scr[...]))
        d1 = jnp.sum(jnp.where(hit1, place, 0.0), axis=1, keepdims=True)
        d2 = jnp.sum(jnp.where(hit2, place, 0.0), axis=1, keepdims=True)
        run_scr[...] = run_scr[...] + jnp.sum(oh, axis=0, keepdims=True)
        dmat = jnp.where(lane == 0, d1, jnp.where(lane == 1, d2, 0.0))
        dest_ref[0] = dmat.T[0:8, :].astype(jnp.int32)
        gate_ref[...] = jnp.where(lane == 0, g0, jnp.where(lane == 1, g1, 0.0))


def _route(logits, tm):
    t = logits.shape[0]
    placed = lambda ph, i: (ph * i, 0)
    return pl.pallas_call(
        _route_kernel,
        grid=(2, t // tm),
        in_specs=[pl.BlockSpec((tm, LANES), lambda ph, i: (i, 0))],
        out_specs=[
            pl.BlockSpec((1, 8, tm), lambda ph, i: (ph * i, 0, 0)),
            pl.BlockSpec((tm, LANES), placed),
            pl.BlockSpec((1, LANES), lambda ph, i: (0, 0)),
        ],
        out_shape=[
            jax.ShapeDtypeStruct((t // tm, 8, tm), jnp.int32),
            jax.ShapeDtypeStruct((t, LANES), F32),
            jax.ShapeDtypeStruct((1, LANES), F32),
        ],
        scratch_shapes=[pltpu.VMEM((1, LANES), F32)] * 3,
        compiler_params=_cparams(("arbitrary", "arbitrary")),
    )(logits)


def _row_copy(src, s_row, dst, d_row, sem):
    return pltpu.make_async_copy(src.at[pl.ds(s_row, 1), :], dst.at[pl.ds(d_row, 1), :], sem)


def _dispatch_kernel(dest_ref, h_ref, xin_ref, xbuf_ref, sem):
    del xin_ref
    tm = h_ref.shape[0]

    def issue(r, c):
        _row_copy(h_ref, r, xbuf_ref, dest_ref[0, 0, r], sem).start()
        _row_copy(h_ref, r, xbuf_ref, dest_ref[0, 1, r], sem).start()
        return c

    lax.fori_loop(0, tm, issue, 0, unroll=DMA_UNROLL)

    def drain(r, c):
        _row_copy(h_ref, 0, xbuf_ref, 0, sem).wait()
        _row_copy(h_ref, 0, xbuf_ref, 0, sem).wait()
        return c

    lax.fori_loop(0, tm, drain, 0, unroll=DMA_UNROLL)


def _dest_spec(tm_route, tm):
    per = tm_route // tm
    return pl.BlockSpec((1, 8, tm), lambda i: (i // per, 0, i % per), memory_space=pltpu.SMEM)


def _dispatch(h, dest, n_rows, tm):
    t, d = h.shape
    xbuf0 = jnp.zeros((n_rows, d), h.dtype)
    return pl.pallas_call(
        _dispatch_kernel,
        grid=(t // tm,),
        in_specs=[
            _dest_spec(dest.shape[2], tm),
            pl.BlockSpec((tm, d), lambda i: (i, 0)),
            pl.BlockSpec(memory_space=pl.ANY),
        ],
        out_specs=pl.BlockSpec(memory_space=pl.ANY),
        out_shape=jax.ShapeDtypeStruct((n_rows, d), h.dtype),
        scratch_shapes=[pltpu.SemaphoreType.DMA(())],
        input_output_aliases={2: 0},
        compiler_params=_cparams(("arbitrary",)),
    )(dest, h, xbuf0)


def _expert_kernel(be_ref, nused_ref, x_ref, wg_ref, wu_ref, wd_ref, y_ref, wg_scr, wu_scr, wd_scr):
    i = pl.program_id(0)
    active = i < nused_ref[0]
    new_expert = (i == 0) | (be_ref[i] != be_ref[jnp.maximum(i - 1, 0)])

    @pl.when(active & new_expert)
    def _():
        wg_scr[...] = wg_ref[0].astype(BF16)
        wu_scr[...] = wu_ref[0].astype(BF16)
        wd_scr[...] = wd_ref[0].astype(BF16)

    @pl.when(active)
    def _():
        x = x_ref[...].astype(BF16)
        gt = jnp.dot(x, wg_scr[...], preferred_element_type=F32)
        up = jnp.dot(x, wu_scr[...], preferred_element_type=F32)
        a = (jax.nn.silu(gt) * up).astype(BF16)
        y_ref[...] = jnp.dot(a, wd_scr[...], preferred_element_type=F32)

    @pl.when(i >= nused_ref[0])
    def _():
        y_ref[...] = jnp.zeros_like(y_ref)


def _experts(xbuf, block_e, n_used, wg_b, wu_b, wd_b):
    n_rows, d = xbuf.shape
    de = wg_b.shape[-1]
    nb = n_rows // MOE_ROWS
    wmap = lambda i, be, nu: (be[i], 0, 0)
    grid_spec = pltpu.PrefetchScalarGridSpec(
        num_scalar_prefetch=2,
        grid=(nb,),
        in_specs=[
            pl.BlockSpec((MOE_ROWS, d), lambda i, be, nu: (i, 0)),
            pl.BlockSpec((1, d, de), wmap),
            pl.BlockSpec((1, d, de), wmap),
            pl.BlockSpec((1, de, d), wmap),
        ],
        out_specs=pl.BlockSpec((MOE_ROWS, d), lambda i, be, nu: (i, 0)),
        scratch_shapes=[pltpu.VMEM((d, de), BF16), pltpu.VMEM((d, de), BF16), pltpu.VMEM((de, d), BF16)],
    )
    return pl.pallas_call(
        _expert_kernel,
        grid_spec=grid_spec,
        out_shape=jax.ShapeDtypeStruct((n_rows, d), F32),
        compiler_params=_cparams(("arbitrary",)),
    )(block_e, n_used, xbuf, wg_b, wu_b, wd_b)


def _combine_kernel(dest_ref, h_ref, gate_ref, g2_ref, b2_ref, ybuf_ref, o_ref, ya_scr, yb_scr, sem):
    tm = h_ref.shape[0]

    def issue(r, c):
        _row_copy(ybuf_ref, dest_ref[0, 0, r], ya_scr, r, sem).start()
        _row_copy(ybuf_ref, dest_ref[0, 1, r], yb_scr, r, sem).start()
        return c

    lax.fori_loop(0, tm, issue, 0, unroll=DMA_UNROLL)

    def drain(r, c):
        _row_copy(ybuf_ref, 0, ya_scr, 0, sem).wait()
        _row_copy(ybuf_ref, 0, yb_scr, 0, sem).wait()
        return c

    lax.fori_loop(0, tm, drain, 0, unroll=DMA_UNROLL)
    gate = gate_ref[...]
    f = ya_scr[...] * gate[:, 0:1] + yb_scr[...] * gate[:, 1:2]
    o_ref[...] = _layer_norm(DN_ALPHA * h_ref[...] + f, g2_ref[...], b2_ref[...])


def _combine(h, gate, dest, ybuf, g2, b2, tm):
    t, d = h.shape
    row = lambda i: (i, 0)
    fix = lambda i: (0, 0)
    return pl.pallas_call(
        _combine_kernel,
        grid=(t // tm,),
        in_specs=[
            _dest_spec(dest.shape[2], tm),
            pl.BlockSpec((tm, d), row),
            pl.BlockSpec((tm, LANES), row),
            pl.BlockSpec((1, d), fix),
            pl.BlockSpec((1, d), fix),
            pl.BlockSpec(memory_space=pl.ANY),
        ],
        out_specs=pl.BlockSpec((tm, d), row),
        out_shape=jax.ShapeDtypeStruct((t, d), F32),
        scratch_shapes=[pltpu.VMEM((tm, d), F32), pltpu.VMEM((tm, d), F32), pltpu.SemaphoreType.DMA(())],
        compiler_params=_cparams(("arbitrary",)),
    )(dest, h, gate, g2, b2, ybuf)


def _moe_finish(h, logits, wg_b, wu_b, wd_b, g2, b2, tm_route, tm_rows):
    t = h.shape[0]
    dest, gate, cnt = _route(logits, tm_route)
    counts = cnt[0, N_EXPERT_GROUPS:N_EXPERT_GROUPS + N_EXPERTS].astype(jnp.int32)
    pad_end = jnp.cumsum((counts + MOE_ROWS - 1) // MOE_ROWS * MOE_ROWS)
    n_blocks = -(-(2 * t) // MOE_ROWS) + N_EXPERTS
    first_row = jnp.arange(n_blocks, dtype=jnp.int32)[:, None] * MOE_ROWS
    block_e = jnp.minimum(jnp.sum((pad_end[None, :] <= first_row).astype(jnp.int32), axis=1),
                          N_EXPERTS - 1)
    n_used = (pad_end[-1:] // MOE_ROWS).astype(jnp.int32)
    xbuf = _dispatch(h, dest, n_blocks * MOE_ROWS, tm_rows)
    ybuf = _experts(xbuf, block_e, n_used, wg_b, wu_b, wd_b)
    return _combine(h, gate, dest, ybuf, g2, b2, tm_rows)


def _pick(t, pref):
    tm = min(pref, t)
    while t % tm:
        tm //= 2
    return tm


def _stream(x, h0_re, h0_im, cache_k, cache_v, params, lam, lam_init):
    (w_in_b, tables, d_row, wglu_b, bglu, rel_bias, g_row, wout_b, g1, b1,
     wr_hi, wr_lo, br, wg_b, wu_b, wd_b, g2, b2) = params
    b, l, d = x.shape
    t = b * l
    x2d = x.reshape(t, d)
    u, k, v, qb, kb, vb, qt, vt = _in_proj(x2d, w_in_b, _pick(t, 512))
    y_ssm, h_re, h_im = _ssm(u, tables, d_row, h0_re, h0_im, l, _pick(t, 4096))
    post = 1.0 - lam_init
    if cache_k is None:
        attn = _attn_prompt(qt, kb, vt, rel_bias, lam, g_row, b, l, post, _pick(l, ATTN_TILE))
    else:
        attn = _attn_sample(qb, kb, vb, cache_k, cache_v, rel_bias, lam, g_row, post)
    h, logits = _mix(x2d, y_ssm, attn, wglu_b, bglu, wout_b, g1, b1, wr_hi, wr_lo, br, _pick(t, 512))
    out = _moe_finish(h, logits, wg_b, wu_b, wd_b, g2, b2, _pick(t, 512), _pick(t, 256))
    return out.reshape(b, l, d), k, v, h_re, h_im


def kernel(x_prompt, x_sample, cache_k, cache_v, state_ssm_re, state_ssm_im, w_in, ssm_a_re, ssm_a_im, ssm_log_dt, ssm_b_re, ssm_b_im, ssm_c_re, ssm_c_im, ssm_d, w_glu, b_glu, lambda_q1, lambda_k1, lambda_q2, lambda_k2, subln_g, rel_bias, w_out, ln1_g, ln1_b, w_r1, b_r1, w_r2, b_r2, w_gate, w_up, w_down, ln2_g, ln2_b):
    assert w_in.shape[0] == DEPTH
    bp, lp, d = x_prompt.shape
    bs, ls, _ = x_sample.shape
    past = cache_k.shape[2]
    nh, dqk = cache_k.shape[3], cache_k.shape[4]
    n_grp, n_state = state_ssm_re.shape[2], state_ssm_re.shape[3]
    l = 0
    lam_init = 0.8 - 0.6 * math.exp(-0.3 * l)
    lam = (jnp.exp(jnp.sum(lambda_q1[l].astype(F32) * lambda_k1[l].astype(F32)))
           - jnp.exp(jnp.sum(lambda_q2[l].astype(F32) * lambda_k2[l].astype(F32))) + lam_init).reshape(1)
    tables = _ssm_tables(ssm_a_re[l].astype(F32), ssm_a_im[l].astype(F32), ssm_log_dt[l].astype(F32),
                         ssm_b_re[l].astype(F32), ssm_b_im[l].astype(F32),
                         ssm_c_re[l].astype(F32), ssm_c_im[l].astype(F32))
    wr = jnp.concatenate([w_r1[l].astype(F32), w_r2[l].astype(F32).reshape(d, N_EXPERTS)], axis=1)
    wr = jnp.pad(wr, ((0, 0), (0, LANES - wr.shape[1])))
    wr_hi = wr.astype(BF16)
    wr_lo = (wr - wr_hi.astype(F32)).astype(BF16)
    br = jnp.concatenate([b_r1[l].astype(F32), b_r2[l].astype(F32).reshape(N_EXPERTS)])
    br = jnp.pad(br, (0, LANES - br.shape[0])).reshape(1, LANES)
    params = (
        w_in[l].astype(BF16), tables, ssm_d[l].astype(F32).reshape(1, -1),
        w_glu[l].astype(BF16), b_glu[l].astype(F32).reshape(1, -1), rel_bias,
        subln_g[l].astype(F32).reshape(1, -1), w_out[l].astype(BF16),
        ln1_g[l].astype(F32).reshape(1, -1), ln1_b[l].astype(F32).reshape(1, -1),
        wr_hi, wr_lo, br,
        w_gate[l], w_up[l], w_down[l],
        ln2_g[l].astype(F32).reshape(1, -1), ln2_b[l].astype(F32).reshape(1, -1),
    )
    zeros = jnp.zeros((bp, n_grp * n_state), F32)
    yp, kp, vp, hp_re, hp_im = _stream(x_prompt, zeros, zeros, None, None, params, lam, lam_init)
    ys, ks, vs, hs_re, hs_im = _stream(
        x_sample, state_ssm_re[l].astype(F32).reshape(bs, -1), state_ssm_im[l].astype(F32).reshape(bs, -1),
        cache_k[l].reshape(bs, past, nh * dqk), cache_v[l].reshape(bs, past, -1), params, lam, lam_init)
    return (yp, ys,
            kp.reshape(1, bp, lp, nh, dqk).astype(cache_k.dtype),
            vp.reshape(1, bp, lp, nh, -1).astype(cache_v.dtype),
            hp_re.reshape(1, bp, n_grp, n_state).astype(state_ssm_re.dtype),
            hp_im.reshape(1, bp, n_grp, n_state).astype(state_ssm_im.dtype),
            ks.reshape(1, bs, ls, nh, dqk).astype(cache_k.dtype),
            vs.reshape(1, bs, ls, nh, -1).astype(cache_v.dtype),
            hs_re.reshape(1, bs, n_grp, n_state).astype(state_ssm_re.dtype),
            hs_im.reshape(1, bs, n_grp, n_state).astype(state_ssm_im.dtype))
```

```python
import functools
import math

import jax
import jax.numpy as jnp
from jax import lax
from jax.experimental import pallas as pl
from jax.experimental.pallas import tpu as pltpu

F32 = jnp.float32
BF16 = jnp.bfloat16

CHUNK = 64
SSM_GROUP_CH = 16
SSM_STATE = 64
N_ATTN_HEADS = 8
ATTN_HEAD_DIM = 64
REL_BUCKETS = 32
REL_MAX_DIST = 128
N_EXPERT_GROUPS = 4
EXPERTS_PER_GROUP = 8
N_EXPERTS = N_EXPERT_GROUPS * EXPERTS_PER_GROUP
DEPTH = 1
DN_ALPHA = (2 * DEPTH) ** 0.25
LN_EPS = 1e-5
RMS_EPS = 1e-5
NEG_INF = -1e30
LOG2E = math.log2(math.e)

LANES = 128
VMEM_LIMIT = 56 * 1024 * 1024

SSM_SUB = 16
SSM_OCT = LANES // SSM_GROUP_CH
ATTN_TILE = 1024
ATTN_COLS = 512
MOE_ROWS = 256
DMA_UNROLL = 8


def _cparams(sem, vmem=VMEM_LIMIT):
    return pltpu.CompilerParams(dimension_semantics=sem, vmem_limit_bytes=vmem)


def _layer_norm(r, g, b):
    mu = jnp.mean(r, axis=-1, keepdims=True)
    rc = r - mu
    var = jnp.mean(rc * rc, axis=-1, keepdims=True)
    return rc * lax.rsqrt(var + LN_EPS) * g + b


def _store_heads_transposed(dst_ref, z):
    for h in range(dst_ref.shape[0]):
        dst_ref[h] = z[:, LANES * h:LANES * (h + 1)].T.astype(BF16)


def _store_heads_split(dst_ref, z):
    for h in range(dst_ref.shape[1]):
        dst_ref[:, h, :] = z[:, LANES * h:LANES * (h + 1)]


def _in_proj_kernel(x_ref, w_ref, u_ref, k_ref, v_ref, qb_ref, kb_ref, vb_ref, qt_ref, vt_ref, xb_scr,
                    *, q_scale):
    j = pl.program_id(1)

    @pl.when(j == 0)
    def _():
        xb_scr[...] = x_ref[...].astype(BF16)

    n = u_ref.shape[1]
    w = w_ref[:, pl.ds(pl.multiple_of(j * n, n), n)]
    z = jnp.dot(xb_scr[...], w, preferred_element_type=F32)

    @pl.when(j == 0)
    def _():
        u_ref[...] = z

    @pl.when(j == 1)
    def _():
        zq = z * q_scale
        qb_ref[...] = zq.astype(BF16)
        _store_heads_transposed(qt_ref, zq * LOG2E)

    @pl.when(j == 2)
    def _():
        _store_heads_split(k_ref, z)
        kb_ref[...] = z.astype(BF16)

    @pl.when(j == 3)
    def _():
        _store_heads_split(v_ref, z)
        vb_ref[...] = z.astype(BF16)
        _store_heads_transposed(vt_ref, z)


def _in_proj(x2d, w_in_b, tm):
    t, d = x2d.shape
    n = w_in_b.shape[1] // 4
    nh = n // LANES
    row = lambda i, j: (i, 0)
    f32_out = jax.ShapeDtypeStruct((t, n), F32)
    b16_out = jax.ShapeDtypeStruct((t, n), BF16)
    tr_out = jax.ShapeDtypeStruct((nh, LANES, t), BF16)
    tr_spec = pl.BlockSpec((nh, LANES, tm), lambda i, j: (0, 0, i))
    cache_out = jax.ShapeDtypeStruct((t, nh, LANES), F32)
    cache_spec = pl.BlockSpec((tm, nh, LANES), lambda i, j: (i, 0, 0))
    return pl.pallas_call(
        functools.partial(_in_proj_kernel, q_scale=ATTN_HEAD_DIM ** -0.5),
        grid=(t // tm, 4),
        in_specs=[pl.BlockSpec((tm, d), row),
                  pl.BlockSpec((d, 4 * n), lambda i, j: (0, 0), pipeline_mode=pl.Buffered(1))],
        out_specs=[pl.BlockSpec((tm, n), row), cache_spec, cache_spec] + [pl.BlockSpec((tm, n), row)] * 3
        + [tr_spec] * 2,
        out_shape=[f32_out, cache_out, cache_out, b16_out, b16_out, b16_out, tr_out, tr_out],
        scratch_shapes=[pltpu.VMEM((tm, d), BF16)],
        compiler_params=_cparams(("arbitrary", "arbitrary")),
    )(x2d, w_in_b)


def _ssm_tables(a_re, a_im, log_dt, b_re, b_im, c_re, c_im):
    hp = lax.Precision.HIGHEST
    g, p = a_re.shape
    c = b_re.shape[-1]
    s, o = SSM_SUB, SSM_OCT
    dt = jnp.exp(log_dt)[:, None]
    mag = jnp.exp(a_re * dt)
    ab_re = mag * jnp.cos(a_im * dt)
    ab_im = mag * jnp.sin(a_im * dt)
    nr = ab_re - 1.0
    ni = ab_im
    den = a_re * a_re + a_im * a_im
    z_re = ((nr * a_re + ni * a_im) / den)[..., None]
    z_im = ((ni * a_re - nr * a_im) / den)[..., None]
    bb_re = z_re * b_re - z_im * b_im
    bb_im = z_re * b_im + z_im * b_re
    n = jnp.arange(s + 1, dtype=F32)[:, None, None]
    pmag = jnp.exp(n * (a_re * dt))
    pw_re = pmag * jnp.cos(n * (a_im * dt))
    pw_im = pmag * jnp.sin(n * (a_im * dt))
    ca_re = c_re[None] * pw_re[:, :, None, :] - c_im[None] * pw_im[:, :, None, :]
    ca_im = c_re[None] * pw_im[:, :, None, :] + c_im[None] * pw_re[:, :, None, :]
    kern = (jnp.einsum('dgop,gpi->dgio', ca_re[:s], bb_re, precision=hp)
            - jnp.einsum('dgop,gpi->dgio', ca_im[:s], bb_im, precision=hp))
    n_oct = g // o

    def expand_groups(tbl, col_group):
        j, n = tbl.shape[1], tbl.shape[3]
        full = jnp.broadcast_to(tbl[:, :, None], (n_oct, j, o, c, n)).reshape(n_oct, j * o * c, n)
        row_l = (lax.broadcasted_iota(jnp.int32, (j * o * c, n), 0) // c) % o
        col_l = col_group(lax.broadcasted_iota(jnp.int32, (j * o * c, n), 1))
        return jnp.where((row_l == col_l)[None], full, 0.0).astype(BF16)

    kt = jnp.transpose(kern.reshape(s, n_oct, o, c, c), (1, 3, 0, 2, 4))
    lags = expand_groups(kt.reshape(n_oct, 1, c, s * o * c), lambda col: (col % LANES) // c)
    m = jnp.concatenate(
        [jnp.pad(lags[:, :, :LANES * (s - j)], ((0, 0), (0, 0), (LANES * j, 0))) for j in range(s)],
        axis=1)
    nrev = (s - 1) - jnp.arange(s, dtype=F32)[:, None, None]
    rmag = jnp.exp(nrev * (a_re * dt))
    rv_re = rmag * jnp.cos(nrev * (a_im * dt))
    rv_im = rmag * jnp.sin(nrev * (a_im * dt))
    wb_re = rv_re[..., None] * bb_re[None] - rv_im[..., None] * bb_im[None]
    wb_im = rv_re[..., None] * bb_im[None] + rv_im[..., None] * bb_re[None]
    state_group = lambda col: (col % (o * p)) // p
    wb = jnp.stack([wb_re, wb_im], axis=0).reshape(2, s, n_oct, o, p, c)
    wt = jnp.transpose(wb, (2, 1, 5, 0, 3, 4)).reshape(n_oct, s, c, 2 * o * p)
    w = expand_groups(wt, state_group)
    zc = jnp.stack([ca_re[1:], -ca_im[1:]], axis=0).reshape(2, s, n_oct, o, c, p)
    zt = jnp.transpose(zc, (2, 1, 4, 0, 3, 5)).reshape(n_oct, s, c, 2 * o * p)
    z_t = expand_groups(zt, state_group)
    return m, w, z_t, a_re * dt, a_im * dt


def _state_powers(exp_re, exp_im, n_pow, n_oct):
    n = (SSM_SUB * jnp.arange(n_pow, dtype=F32))[:, None, None]
    mag = jnp.exp(n * exp_re)
    pw_re = (mag * jnp.cos(n * exp_im)).reshape(n_pow, n_oct, -1)
    pw_im = (mag * jnp.sin(n * exp_im)).reshape(n_pow, n_oct, -1)
    return jnp.swapaxes(pw_re, 0, 1), jnp.swapaxes(pw_im, 0, 1)


def _ssm_kernel(u_ref, m_ref, w_ref, zt_ref, pre_ref, pim_ref, d_ref, h0re_ref, h0im_ref,
                y_ref, hre_ref, him_ref, lhs_scr, v_scr, sin_scr, cre_scr, cim_scr, *, rows, cps):
    i = pl.program_id(1)
    half = cre_scr.shape[1]

    @pl.when(i == 0)
    def _():
        cre_scr[...] = jnp.zeros_like(cre_scr)
        cim_scr[...] = jnp.zeros_like(cim_scr)

    for j in range(SSM_SUB):
        piece = u_ref[pl.ds(j, rows, stride=SSM_SUB), :]
        lhs_scr[:, LANES * j:LANES * (j + 1)] = piece.astype(BF16)
    lhs = lhs_scr[...]
    nt = half // LANES
    v = jnp.dot(lhs, w_ref[0], preferred_element_type=F32)
    for c in range(2 * nt):
        v_scr[c] = v[:, LANES * c:LANES * (c + 1)]
    seg = rows // 8
    a_re = pre_ref[0, 1:2, :]
    a_im = pim_ref[0, 1:2, :]

    def tiles(ref, r, part):
        return jnp.concatenate([ref[part * nt + c, r, :] for c in range(nt)], axis=1)

    def put_tiles(ref, r, part, val):
        for c in range(nt):
            ref[part * nt + c, r, :] = val[:, LANES * c:LANES * (c + 1)]

    def local(t, carry):
        l_re, l_im = carry
        r = pl.ds(t, 8, stride=seg)
        put_tiles(sin_scr, r, 0, l_re)
        put_tiles(sin_scr, r, 1, l_im)
        v_re = tiles(v_scr, r, 0)
        v_im = tiles(v_scr, r, 1)
        return a_re * l_re - a_im * l_im + v_re, a_re * l_im + a_im * l_re + v_im

    zero = jnp.zeros((8, half), F32)
    e_re, e_im = lax.fori_loop(0, seg, local, (zero, zero), unroll=4)

    aseg_re = pre_ref[0, seg:seg + 1, :]
    aseg_im = pim_ref[0, seg:seg + 1, :]
    c_re = cre_scr[...]
    c_im = cim_scr[...]
    ins_re, ins_im = [], []
    for s in range(8):
        first = i * rows + s * seg
        seq = first // cps
        start = (first % cps) == 0
        c_re = jnp.where(start, h0re_ref[pl.ds(seq, 1), :], c_re)
        c_im = jnp.where(start, h0im_ref[pl.ds(seq, 1), :], c_im)
        ins_re.append(c_re)
        ins_im.append(c_im)
        n_re = aseg_re * c_re - aseg_im * c_im + e_re[s:s + 1]
        n_im = aseg_re * c_im + aseg_im * c_re + e_im[s:s + 1]
        hre_ref[pl.ds(seq, 1), :] = n_re
        him_ref[pl.ds(seq, 1), :] = n_im
        c_re, c_im = n_re, n_im
    cre_scr[...] = c_re
    cim_scr[...] = c_im
    in_re = jnp.concatenate(ins_re, axis=0)
    in_im = jnp.concatenate(ins_im, axis=0)

    def carry_in(t, c):
        r = pl.ds(t, 8, stride=seg)
        p_re = pre_ref[0, pl.ds(t, 1), :]
        p_im = pim_ref[0, pl.ds(t, 1), :]
        put_tiles(sin_scr, r, 0, tiles(sin_scr, r, 0) + (p_re * in_re - p_im * in_im))
        put_tiles(sin_scr, r, 1, tiles(sin_scr, r, 1) + (p_re * in_im + p_im * in_re))
        return c

    lax.fori_loop(0, seg, carry_in, 0, unroll=4)

    s_in = jnp.concatenate([sin_scr[c] for c in range(2 * nt)], axis=1).astype(BF16)
    y = (jnp.dot(lhs, m_ref[0], preferred_element_type=F32)
         + lax.dot_general(s_in, zt_ref[0], (((1,), (1,)), ((), ())), preferred_element_type=F32))
    d = d_ref[...]
    for j in range(SSM_SUB):
        uj = u_ref[pl.ds(j, rows, stride=SSM_SUB), :]
        y_ref[pl.ds(j, rows, stride=SSM_SUB), :] = y[:, LANES * j:LANES * (j + 1)] + d * uj


def _ssm(u, tables, d_row, h0_re, h0_im, seq_len, tm):
    m, w, z, exp_re, exp_im = tables
    t, width = u.shape
    n_oct = width // LANES
    nseq = h0_re.shape[0]
    rows = tm // SSM_SUB
    cps = seq_len // SSM_SUB
    seg = rows // 8
    assert rows % 8 == 0 and cps % seg == 0
    pw_re, pw_im = _state_powers(exp_re, exp_im, seg + 1, n_oct)
    half = pw_re.shape[-1]
    kdim = SSM_SUB * LANES
    once = pl.Buffered(1)
    col = lambda g, i: (i, g)
    per_oct3 = lambda g, i: (g, 0, 0)
    per_oct2 = lambda g, i: (0, g)
    return pl.pallas_call(
        functools.partial(_ssm_kernel, rows=rows, cps=cps),
        grid=(n_oct, t // tm),
        in_specs=[
            pl.BlockSpec((tm, LANES), col),
            pl.BlockSpec((1, kdim, kdim), per_oct3, pipeline_mode=once),
            pl.BlockSpec((1, kdim, 2 * half), per_oct3, pipeline_mode=once),
            pl.BlockSpec((1, kdim, 2 * half), per_oct3, pipeline_mode=once),
            pl.BlockSpec((1, seg + 1, half), per_oct3),
            pl.BlockSpec((1, seg + 1, half), per_oct3),
            pl.BlockSpec((1, LANES), per_oct2),
            pl.BlockSpec((nseq, half), per_oct2),
            pl.BlockSpec((nseq, half), per_oct2),
        ],
        out_specs=[
            pl.BlockSpec((tm, LANES), col),
            pl.BlockSpec((nseq, half), per_oct2),
            pl.BlockSpec((nseq, half), per_oct2),
        ],
        out_shape=[
            jax.ShapeDtypeStruct((t, width), F32),
            jax.ShapeDtypeStruct((nseq, n_oct * half), F32),
            jax.ShapeDtypeStruct((nseq, n_oct * half), F32),
        ],
        scratch_shapes=[
            pltpu.VMEM((rows, kdim), BF16),
            pltpu.VMEM((2 * half // LANES, rows, LANES), F32),
            pltpu.VMEM((2 * half // LANES, rows, LANES), F32),
            pltpu.VMEM((1, half), F32),
            pltpu.VMEM((1, half), F32),
        ],
        compiler_params=_cparams(("arbitrary", "arbitrary")),
    )(u, m, w, z, pw_re, pw_im, d_row, h0_re, h0_im)


def _rel_bucket(rel):
    nb = REL_BUCKETS // 2
    max_exact = nb // 2
    n = jnp.abs(rel)
    nf = jnp.maximum(n, 1).astype(F32)
    large = max_exact + (jnp.log(nf / max_exact) / math.log(REL_MAX_DIST / max_exact)
                         * (nb - max_exact)).astype(jnp.int32)
    large = jnp.minimum(large, nb - 1)
    return jnp.where(rel > 0, nb, 0) + jnp.where(n < max_exact, n, large)


def _bias_table(rel_bias, n_k, n_q, q0, shift):
    k_pos = lax.broadcasted_iota(jnp.int32, (n_k, n_q), 0)
    q_pos = q0 + lax.broadcasted_iota(jnp.int32, (n_k, n_q), 1)
    bucket = _rel_bucket(k_pos - q_pos)[None]
    rows = (rel_bias.astype(F32) - shift[None, :])[:, :, None, None]
    bias = jnp.broadcast_to(rows[0], (rel_bias.shape[1], n_k, n_q))
    for b in range(1, REL_BUCKETS):
        bias = jnp.where(bucket == b, rows[b], bias)
    return jnp.where((k_pos // CHUNK <= q_pos // CHUNK)[None], bias, NEG_INF)


def _split_heads_lhs(q):
    lane = lax.broadcasted_iota(jnp.int32, q.shape, 1)
    zero = jnp.zeros_like(q)
    return jnp.concatenate([jnp.where(lane < ATTN_HEAD_DIM, q, zero),
                            jnp.where(lane >= ATTN_HEAD_DIM, q, zero)], axis=0)


def _nt_dot(a, b):
    return lax.dot_general(a, b, (((1,), (1,)), ((), ())), preferred_element_type=F32)


def _diff_finish(acc, l, lam, g, tq, post_scale):
    o = acc[:tq] / l[:tq] - lam * (acc[tq:] / l[tq:])
    ms = jnp.mean(o * o, axis=-1, keepdims=True)
    return o * lax.rsqrt(ms + RMS_EPS) * g * post_scale


def _flash_kernel(lam_ref, qt_ref, k_ref, vt_ref, bias_ref, g_ref, o_ref,
                  lhs_scr, s0_scr, s1_scr, m_scr, acc_scr, *, tq, tk, cb, post_scale):
    qi = pl.program_id(2)
    qt = qt_ref[0]
    row = lax.broadcasted_iota(jnp.int32, qt.shape, 0)
    zero = jnp.zeros_like(qt)
    lhs_scr[:, 0:tq] = jnp.where(row < ATTN_HEAD_DIM, qt, zero)
    lhs_scr[:, tq:2 * tq] = jnp.where(row >= ATTN_HEAD_DIM, qt, zero)
    m_scr[...] = jnp.full_like(m_scr, NEG_INF)
    acc_scr[...] = jnp.zeros_like(acc_scr)
    ones_rows = jnp.ones((acc_scr.shape[0] - LANES, tk), BF16)
    s_bufs = (s0_scr, s1_scr)

    all_blocks = tuple(range(2 * tq // cb))
    late_blocks = tuple(c for c in all_blocks if (c * cb) % tq >= tk)

    def qk(tile, buf, blocks=None):
        k = k_ref[pl.ds(pl.multiple_of(tile * tk, tk), tk), :]
        if blocks is None:
            s_bufs[buf][...] = jnp.dot(k, lhs_scr[...], preferred_element_type=F32)
        else:
            for c in blocks:
                cols = pl.ds(c * cb, cb)
                s_bufs[buf][:, cols] = jnp.dot(k, lhs_scr[:, cols], preferred_element_type=F32)

    def softmax_pv(tile, buf, bias, blocks=all_blocks):
        vt = vt_ref[0, :, pl.ds(pl.multiple_of(tile * tk, tk), tk)]
        vt = jnp.concatenate([vt, ones_rows], axis=0)
        for c in blocks:
            cols = pl.ds(c * cb, cb)
            s = s_bufs[buf][:, cols]
            if bias is not None:
                b0 = (c * cb) % tq
                s = s + bias[:, b0:b0 + cb]
            m_prev = m_scr[:, cols]
            m_new = jnp.maximum(m_prev, jnp.max(s, axis=0, keepdims=True))
            alpha = jnp.exp2(m_prev - m_new)
            p = jnp.exp2(s - m_new).astype(BF16)
            acc_scr[:, cols] = alpha * acc_scr[:, cols] + jnp.dot(vt, p, preferred_element_type=F32)
            m_scr[:, cols] = m_new

    qk(0, 0)

    def far_pair(p, carry):
        qk(2 * p + 1, 1)
        softmax_pv(2 * p, 0, None)
        qk(2 * p + 2, 0)
        softmax_pv(2 * p + 1, 1, None)
        return carry

    lax.fori_loop(0, jnp.maximum(qi - 1, 0), far_pair, 0)

    @pl.when(qi > 0)
    def _():
        qk(2 * qi - 1, 1)
        softmax_pv(2 * qi - 2, 0, None)
        qk(2 * qi, 0)
        softmax_pv(2 * qi - 1, 1, bias_ref[0, 0])

    qk(2 * qi + 1, 1, late_blocks)
    softmax_pv(2 * qi, 0, bias_ref[0, 1])
    softmax_pv(2 * qi + 1, 1, bias_ref[0, 2], late_blocks)
    acc = acc_scr[...]
    l = acc[LANES:LANES + 1]
    o = acc[:LANES, :tq] / l[:, :tq] - lam_ref[0] * (acc[:LANES, tq:] / l[:, tq:])
    ms = jnp.mean(o * o, axis=0, keepdims=True)
    o = o * lax.rsqrt(ms + RMS_EPS) * (g_ref[...] * post_scale)
    o_ref[...] = o.T.astype(o_ref.dtype)


def _attn_prompt(qt, kb, vt, rel_bias, lam, g, batch, seq_len, post_scale, tq):
    tk = tq // 2
    assert tk >= REL_MAX_DIST and tk % CHUNK == 0 and seq_len % tq == 0
    nh, _, t = qt.shape
    nq = seq_len // tq
    far_bias = rel_bias.astype(F32)[_rel_bucket(jnp.int32(-(tk + 1)))]
    table = (_bias_table(rel_bias, 3 * tk, tq, tk, far_bias) * LOG2E).reshape(nh, 3, tk, tq)
    g_col = jnp.broadcast_to(g.reshape(LANES, 1), (LANES, tq))
    acc_rows = LANES + 16
    return pl.pallas_call(
        functools.partial(_flash_kernel, tq=tq, tk=tk, cb=ATTN_COLS, post_scale=post_scale),
        grid=(batch, nh, nq),
        in_specs=[
            pl.BlockSpec(memory_space=pltpu.SMEM),
            pl.BlockSpec((1, LANES, tq), lambda b, h, qi: (h, 0, b * nq + qi)),
            pl.BlockSpec((seq_len, LANES), lambda b, h, qi: (b, h)),
            pl.BlockSpec((1, LANES, seq_len), lambda b, h, qi: (h, 0, b)),
            pl.BlockSpec((1, 3, tk, tq), lambda b, h, qi: (h, 0, 0, 0)),
            pl.BlockSpec((LANES, tq), lambda b, h, qi: (0, 0)),
        ],
        out_specs=pl.BlockSpec((tq, LANES), lambda b, h, qi: (b * nq + qi, h)),
        out_shape=jax.ShapeDtypeStruct((t, nh * LANES), BF16),
        scratch_shapes=[
            pltpu.VMEM((LANES, 2 * tq), BF16),
            pltpu.VMEM((tk, 2 * tq), F32),
            pltpu.VMEM((tk, 2 * tq), F32),
            pltpu.VMEM((1, 2 * tq), F32),
            pltpu.VMEM((acc_rows, 2 * tq), F32),
        ],
        compiler_params=_cparams(("arbitrary", "arbitrary", "arbitrary")),
    )(lam, qt, kb, vt, table, g_col)


def _sample_attn_kernel(lam_ref, q_ref, ck_ref, cv_ref, kn_ref, vn_ref, bias_ref, g_ref, o_ref,
                        *, past, post_scale):
    tq = q_ref.shape[0]
    lhs = _split_heads_lhs(q_ref[...])
    bias = bias_ref[0]
    bias2 = jnp.concatenate([bias, bias], axis=0)
    s_c = _nt_dot(lhs, ck_ref[0].astype(BF16)) + bias2[:, :past]
    s_n = _nt_dot(lhs, kn_ref[...]) + bias2[:, past:]
    m = jnp.maximum(jnp.max(s_c, axis=1, keepdims=True), jnp.max(s_n, axis=1, keepdims=True))
    p_c = jnp.exp(s_c - m)
    p_n = jnp.exp(s_n - m)
    l = jnp.sum(p_c, axis=1, keepdims=True) + jnp.sum(p_n, axis=1, keepdims=True)
    acc = (jnp.dot(p_c.astype(BF16), cv_ref[0].astype(BF16), preferred_element_type=F32)
           + jnp.dot(p_n.astype(BF16), vn_ref[...], preferred_element_type=F32))
    o = _diff_finish(acc, l, lam_ref[0], g_ref[...], tq, post_scale)
    o_ref[...] = o.astype(o_ref.dtype)


def _attn_sample(qb, kb, vb, cache_k, cache_v, rel_bias, lam, g_row, post_scale):
    nstream, past, width = cache_k.shape
    nh = width // LANES
    ls = qb.shape[0] // nstream
    no_shift = jnp.zeros((nh,), F32)
    table = jnp.swapaxes(_bias_table(rel_bias, past + ls, ls, past, no_shift), 1, 2)
    new = lambda b, h: (b, h)
    old = lambda b, h: (b, 0, h)
    return pl.pallas_call(
        functools.partial(_sample_attn_kernel, past=past, post_scale=post_scale),
        grid=(nstream, nh),
        in_specs=[
            pl.BlockSpec(memory_space=pltpu.SMEM),
            pl.BlockSpec((ls, LANES), new),
            pl.BlockSpec((1, past, LANES), old),
            pl.BlockSpec((1, past, LANES), old),
            pl.BlockSpec((ls, LANES), new),
            pl.BlockSpec((ls, LANES), new),
            pl.BlockSpec((1, ls, past + ls), lambda b, h: (h, 0, 0)),
            pl.BlockSpec((1, LANES), lambda b, h: (0, 0)),
        ],
        out_specs=pl.BlockSpec((ls, LANES), new),
        out_shape=jax.ShapeDtypeStruct(qb.shape, BF16),
        compiler_params=_cparams(("arbitrary", "arbitrary")),
    )(lam, qb, cache_k, cache_v, kb, vb, table, g_row)


def _mix_kernel(x_ref, y_ref, a_ref, wglu_ref, bglu_ref, wout_ref, g1_ref, b1_ref,
                wrh_ref, wr2_ref, br_ref, h_ref, hp_ref, lg_ref):
    half = y_ref.shape[1]
    g = jax.nn.gelu(y_ref[...])
    t = jnp.dot(g.astype(BF16), wglu_ref[...], preferred_element_type=F32) + bglu_ref[...]
    so = g * jax.nn.sigmoid(t)
    mix = (jnp.dot(so.astype(BF16), wout_ref[0:half, :], preferred_element_type=F32)
           + jnp.dot(a_ref[...], wout_ref[half:2 * half, :], preferred_element_type=F32))
    h = _layer_norm(DN_ALPHA * x_ref[...] + mix, g1_ref[...], b1_ref[...])
    h_ref[...] = h
    h_hi = h.astype(BF16)
    h_rt = h_hi.astype(F32)
    top = lax.bitcast_convert_type(h_rt[:, :half], jnp.uint32)
    bot = lax.bitcast_convert_type(h_rt[:, half:], jnp.uint32)
    hp_ref[...] = top | (bot >> 16)
    h_lo = (h - h_rt).astype(BF16)
    both = jnp.dot(h_hi, wr2_ref[...], preferred_element_type=F32)
    lg = both[:, :LANES] + both[:, LANES:] + jnp.dot(h_lo, wrh_ref[...], preferred_element_type=F32)
    lg_ref[...] = lg + br_ref[...]


def _mix(x2d, y_ssm, attn, wglu_b, bglu, wout_b, g1, b1, wr_hi, wr2, br, tm):
    t, d = x2d.shape
    half = d // 2
    row = lambda i: (i, 0)
    fix = lambda i: (0, 0)
    once = pl.Buffered(1)
    return pl.pallas_call(
        _mix_kernel,
        grid=(t // tm,),
        in_specs=[
            pl.BlockSpec((tm, d), row),
            pl.BlockSpec((tm, half), row),
            pl.BlockSpec((tm, half), row),
            pl.BlockSpec((half, half), fix, pipeline_mode=once),
            pl.BlockSpec((1, half), fix),
            pl.BlockSpec((d, d), fix, pipeline_mode=once),
            pl.BlockSpec((1, d), fix),
            pl.BlockSpec((1, d), fix),
            pl.BlockSpec((d, LANES), fix),
            pl.BlockSpec((d, 2 * LANES), fix),
            pl.BlockSpec((1, LANES), fix),
        ],
        out_specs=[pl.BlockSpec((tm, d), row), pl.BlockSpec((tm, half), row), pl.BlockSpec((tm, LANES), row)],
        out_shape=[jax.ShapeDtypeStruct((t, d), F32), jax.ShapeDtypeStruct((t, half), jnp.uint32),
                   jax.ShapeDtypeStruct((t, LANES), F32)],
        compiler_params=_cparams(("arbitrary",)),
    )(x2d, y_ssm, attn, wglu_b, bglu, wout_b, g1, b1, wr_hi, wr2, br)


def _route_kernel(lg_ref, dest_ref, gate_ref, cnt_ref, tot_scr, run_scr, start_scr):
    ph = pl.program_id(0)
    i = pl.program_id(1)
    tm = lg_ref.shape[0]
    ng = N_EXPERT_GROUPS

    @pl.when((ph == 0) & (i == 0))
    def _():
        tot_scr[...] = jnp.zeros_like(tot_scr)

    lg = lg_ref[...]
    lane = lax.broadcasted_iota(jnp.int32, lg.shape, 1)
    is_g = lane < ng
    l1 = jnp.where(is_g, lg, -jnp.inf)
    m1 = jnp.max(l1, axis=1, keepdims=True)
    z1 = jnp.sum(jnp.where(is_g, jnp.exp(l1 - m1), 0.0), axis=1, keepdims=True)
    w_grp = 1.0 / z1
    el = (lane - ng).astype(F32)
    none = float(LANES)
    grp = jnp.min(jnp.where(l1 == m1, lane.astype(F32), none), axis=1, keepdims=True)
    lo = grp * EXPERTS_PER_GROUP
    in_grp = jnp.where(el >= lo, jnp.where(el < lo + EXPERTS_PER_GROUP, 1.0, 0.0), 0.0) > 0.5
    l2 = jnp.where(in_grp, lg, -jnp.inf)
    t1 = jnp.max(l2, axis=1, keepdims=True)
    i1 = jnp.min(jnp.where(l2 == t1, el, none), axis=1, keepdims=True)
    l2b = jnp.where(el == i1, -jnp.inf, l2)
    t2 = jnp.max(l2b, axis=1, keepdims=True)
    i2 = jnp.min(jnp.where(l2b == t2, el, none), axis=1, keepdims=True)
    e2 = jnp.exp(t2 - t1)
    den = 1.0 + e2
    g0 = w_grp * (1.0 / den)
    g1 = w_grp * (e2 / den)
    hit1 = el == i1
    hit2 = el == i2
    oh = jnp.where(hit1, 1.0, jnp.where(hit2, 1.0, 0.0))

    @pl.when(ph == 0)
    def _():
        tot_scr[...] = tot_scr[...] + jnp.sum(oh, axis=0, keepdims=True)

    @pl.when(ph == 1)
    def _():
        @pl.when(i == 0)
        def _():
            cnt = tot_scr[...]
            padded = jnp.floor((cnt + (MOE_ROWS - 1)) * (1.0 / MOE_ROWS)) * MOE_ROWS
            padded = jnp.broadcast_to(padded, (8, LANES))
            lane8 = lax.broadcasted_iota(jnp.int32, padded.shape, 1)
            ends = padded
            for sh in (1, 2, 4, 8, 16, 32):
                ends = ends + jnp.where(lane8 >= sh, pltpu.roll(ends, sh, axis=1), 0.0)
            start_scr[...] = (ends - padded)[0:1]
            run_scr[...] = jnp.zeros_like(run_scr)
            cnt_ref[...] = cnt

        r_id = lax.broadcasted_iota(jnp.int32, (tm, tm), 0)
        c_id = lax.broadcasted_iota(jnp.int32, (tm, tm), 1)
        tri = jnp.where(c_id < r_id, 1.0, 0.0).astype(BF16)
        place = (jnp.dot(tri, oh.astype(BF16), preferred_element_type=F32)
                 + (run_scr[...] + start_scr[...]))
        d1 = jnp.sum(jnp.where(hit1, place, 0.0), axis=1, keepdims=True)
        d2 = jnp.sum(jnp.where(hit2, place, 0.0), axis=1, keepdims=True)
        run_scr[...] = run_scr[...] + jnp.sum(oh, axis=0, keepdims=True)
        dmat = jnp.where(lane == 0, d1, jnp.where(lane == 1, d2, 0.0))
        dest_ref[0] = dmat.T[0:8, :].astype(jnp.int32)
        gate_ref[...] = jnp.where(lane == 0, g0, jnp.where(lane == 1, g1, 0.0))


def _route(logits, tm):
    t = logits.shape[0]
    placed = lambda ph, i: (ph * i, 0)
    return pl.pallas_call(
        _route_kernel,
        grid=(2, t // tm),
        in_specs=[pl.BlockSpec((tm, LANES), lambda ph, i: (i, 0))],
        out_specs=[
            pl.BlockSpec((1, 8, tm), lambda ph, i: (ph * i, 0, 0)),
            pl.BlockSpec((tm, LANES), placed),
            pl.BlockSpec((1, LANES), lambda ph, i: (0, 0)),
        ],
        out_shape=[
            jax.ShapeDtypeStruct((t // tm, 8, tm), jnp.int32),
            jax.ShapeDtypeStruct((t, LANES), F32),
            jax.ShapeDtypeStruct((1, LANES), F32),
        ],
        scratch_shapes=[pltpu.VMEM((1, LANES), F32)] * 3,
        compiler_params=_cparams(("arbitrary", "arbitrary")),
    )(logits)


def _row_copy(src, s_row, dst, d_row, sem):
    return pltpu.make_async_copy(src.at[pl.ds(s_row, 1), :], dst.at[pl.ds(d_row, 1), :], sem)


def _dispatch_kernel(dest_ref, h_ref, xin_ref, xbuf_ref, sem):
    del xin_ref
    tm = h_ref.shape[0]

    def issue(r, c):
        _row_copy(h_ref, r, xbuf_ref, dest_ref[0, 0, r], sem).start()
        _row_copy(h_ref, r, xbuf_ref, dest_ref[0, 1, r], sem).start()
        return c

    lax.fori_loop(0, tm, issue, 0, unroll=DMA_UNROLL)

    def drain(r, c):
        _row_copy(h_ref, 0, xbuf_ref, 0, sem).wait()
        _row_copy(h_ref, 0, xbuf_ref, 0, sem).wait()
        return c

    lax.fori_loop(0, tm, drain, 0, unroll=DMA_UNROLL)


def _dest_spec(tm_route, tm):
    per = tm_route // tm
    return pl.BlockSpec((1, 8, tm), lambda i: (i // per, 0, i % per), memory_space=pltpu.SMEM)


def _dispatch(h, dest, n_rows, tm):
    t, d = h.shape
    xbuf0 = jnp.zeros((n_rows, d), h.dtype)
    return pl.pallas_call(
        _dispatch_kernel,
        grid=(t // tm,),
        in_specs=[
            _dest_spec(dest.shape[2], tm),
            pl.BlockSpec((tm, d), lambda i: (i, 0)),
            pl.BlockSpec(memory_space=pl.ANY),
        ],
        out_specs=pl.BlockSpec(memory_space=pl.ANY),
        out_shape=jax.ShapeDtypeStruct((n_rows, d), h.dtype),
        scratch_shapes=[pltpu.SemaphoreType.DMA(())],
        input_output_aliases={2: 0},
        compiler_params=_cparams(("arbitrary",)),
    )(dest, h, xbuf0)


def _expert_kernel(be_ref, nused_ref, x_ref, wg_ref, wu_ref, wd_ref, y_ref, wg_scr, wu_scr, wd_scr):
    i = pl.program_id(0)
    active = i < nused_ref[0]
    new_expert = (i == 0) | (be_ref[i] != be_ref[jnp.maximum(i - 1, 0)])

    @pl.when(active & new_expert)
    def _():
        wg_scr[...] = wg_ref[0].astype(BF16)
        wu_scr[...] = wu_ref[0].astype(BF16)
        wd_scr[...] = wd_ref[0].astype(BF16)

    @pl.when(active)
    def _():
        xp = x_ref[...]
        top = lax.bitcast_convert_type(xp & jnp.uint32(0xFFFF0000), F32).astype(BF16)
        bot = lax.bitcast_convert_type(xp << 16, F32).astype(BF16)
        x = jnp.concatenate([top, bot], axis=1)
        gt = jnp.dot(x, wg_scr[...], preferred_element_type=F32)
        up = jnp.dot(x, wu_scr[...], preferred_element_type=F32)
        a = (jax.nn.silu(gt) * up).astype(BF16)
        y_ref[...] = jnp.dot(a, wd_scr[...], preferred_element_type=F32)

    @pl.when(i >= nused_ref[0])
    def _():
        y_ref[...] = jnp.zeros_like(y_ref)


def _experts(xbuf, block_e, n_used, wg_b, wu_b, wd_b):
    n_rows = xbuf.shape[0]
    d, de = wg_b.shape[1], wg_b.shape[2]
    nb = n_rows // MOE_ROWS
    wmap = lambda i, be, nu: (be[i], 0, 0)
    grid_spec = pltpu.PrefetchScalarGridSpec(
        num_scalar_prefetch=2,
        grid=(nb,),
        in_specs=[
            pl.BlockSpec((MOE_ROWS, d // 2), lambda i, be, nu: (i, 0)),
            pl.BlockSpec((1, d, de), wmap),
            pl.BlockSpec((1, d, de), wmap),
            pl.BlockSpec((1, de, d), wmap),
        ],
        out_specs=pl.BlockSpec((MOE_ROWS, d), lambda i, be, nu: (i, 0)),
        scratch_shapes=[pltpu.VMEM((d, de), BF16), pltpu.VMEM((d, de), BF16), pltpu.VMEM((de, d), BF16)],
    )
    return pl.pallas_call(
        _expert_kernel,
        grid_spec=grid_spec,
        out_shape=jax.ShapeDtypeStruct((n_rows, d), F32),
        compiler_params=_cparams(("arbitrary",)),
    )(block_e, n_used, xbuf, wg_b, wu_b, wd_b)


def _combine_kernel(dest_ref, h_ref, gate_ref, g2_ref, b2_ref, ybuf_ref, o_ref, ya_scr, yb_scr, sem):
    tm = h_ref.shape[0]

    def issue(r, c):
        _row_copy(ybuf_ref, dest_ref[0, 0, r], ya_scr, r, sem).start()
        _row_copy(ybuf_ref, dest_ref[0, 1, r], yb_scr, r, sem).start()
        return c

    lax.fori_loop(0, tm, issue, 0, unroll=DMA_UNROLL)

    def drain(r, c):
        _row_copy(ybuf_ref, 0, ya_scr, 0, sem).wait()
        _row_copy(ybuf_ref, 0, yb_scr, 0, sem).wait()
        return c

    lax.fori_loop(0, tm, drain, 0, unroll=DMA_UNROLL)
    gate = gate_ref[...]
    f = ya_scr[...] * gate[:, 0:1] + yb_scr[...] * gate[:, 1:2]
    o_ref[...] = _layer_norm(DN_ALPHA * h_ref[...] + f, g2_ref[...], b2_ref[...])


def _combine(h, gate, dest, ybuf, g2, b2, tm):
    t, d = h.shape
    row = lambda i: (i, 0)
    fix = lambda i: (0, 0)
    return pl.pallas_call(
        _combine_kernel,
        grid=(t // tm,),
        in_specs=[
            _dest_spec(dest.shape[2], tm),
            pl.BlockSpec((tm, d), row),
            pl.BlockSpec((tm, LANES), row),
            pl.BlockSpec((1, d), fix),
            pl.BlockSpec((1, d), fix),
            pl.BlockSpec(memory_space=pl.ANY),
        ],
        out_specs=pl.BlockSpec((tm, d), row),
        out_shape=jax.ShapeDtypeStruct((t, d), F32),
        scratch_shapes=[pltpu.VMEM((tm, d), F32), pltpu.VMEM((tm, d), F32), pltpu.SemaphoreType.DMA(())],
        compiler_params=_cparams(("arbitrary",)),
    )(dest, h, gate, g2, b2, ybuf)


def _moe_finish(h, hp, logits, wg_b, wu_b, wd_b, g2, b2, tm_route, tm_rows):
    t = h.shape[0]
    dest, gate, cnt = _route(logits, tm_route)
    counts = cnt[0, N_EXPERT_GROUPS:N_EXPERT_GROUPS + N_EXPERTS].astype(jnp.int32)
    pad_end = jnp.cumsum((counts + MOE_ROWS - 1) // MOE_ROWS * MOE_ROWS)
    n_blocks = -(-(2 * t) // MOE_ROWS) + N_EXPERTS
    first_row = jnp.arange(n_blocks, dtype=jnp.int32)[:, None] * MOE_ROWS
    block_e = jnp.minimum(jnp.sum((pad_end[None, :] <= first_row).astype(jnp.int32), axis=1),
                          N_EXPERTS - 1)
    n_used = (pad_end[-1:] // MOE_ROWS).astype(jnp.int32)
    xbuf = _dispatch(hp, dest, n_blocks * MOE_ROWS, tm_rows)
    ybuf = _experts(xbuf, block_e, n_used, wg_b, wu_b, wd_b)
    return _combine(h, gate, dest, ybuf, g2, b2, tm_rows)


def _pick(t, pref):
    tm = min(pref, t)
    while t % tm:
        tm //= 2
    return tm


def _stream(x, h0_re, h0_im, cache_k, cache_v, params, lam, lam_init):
    (w_in_b, tables, d_row, wglu_b, bglu, rel_bias, g_row, wout_b, g1, b1,
     wr_hi, wr2, br, wg_b, wu_b, wd_b, g2, b2) = params
    b, l, d = x.shape
    t = b * l
    x2d = x.reshape(t, d)
    u, k, v, qb, kb, vb, qt, vt = _in_proj(x2d, w_in_b, _pick(t, 512))
    y_ssm, h_re, h_im = _ssm(u, tables, d_row, h0_re, h0_im, l, _pick(t, 4096))
    post = 1.0 - lam_init
    if cache_k is None:
        attn = _attn_prompt(qt, kb, vt, rel_bias, lam, g_row, b, l, post, _pick(l, ATTN_TILE))
    else:
        attn = _attn_sample(qb, kb, vb, cache_k, cache_v, rel_bias, lam, g_row, post)
    h, hp, logits = _mix(x2d, y_ssm, attn, wglu_b, bglu, wout_b, g1, b1, wr_hi, wr2, br, _pick(t, 512))
    out = _moe_finish(h, hp, logits, wg_b, wu_b, wd_b, g2, b2, _pick(t, 512), _pick(t, 256))
    return out.reshape(b, l, d), k, v, h_re, h_im


def kernel(x_prompt, x_sample, cache_k, cache_v, state_ssm_re, state_ssm_im, w_in, ssm_a_re, ssm_a_im, ssm_log_dt, ssm_b_re, ssm_b_im, ssm_c_re, ssm_c_im, ssm_d, w_glu, b_glu, lambda_q1, lambda_k1, lambda_q2, lambda_k2, subln_g, rel_bias, w_out, ln1_g, ln1_b, w_r1, b_r1, w_r2, b_r2, w_gate, w_up, w_down, ln2_g, ln2_b):
    assert w_in.shape[0] == DEPTH
    bp, lp, d = x_prompt.shape
    bs, ls, _ = x_sample.shape
    past = cache_k.shape[2]
    nh, dqk = cache_k.shape[3], cache_k.shape[4]
    n_grp, n_state = state_ssm_re.shape[2], state_ssm_re.shape[3]
    l = 0
    lam_init = 0.8 - 0.6 * math.exp(-0.3 * l)
    lam = (jnp.exp(jnp.sum(lambda_q1[l].astype(F32) * lambda_k1[l].astype(F32)))
           - jnp.exp(jnp.sum(lambda_q2[l].astype(F32) * lambda_k2[l].astype(F32))) + lam_init).reshape(1)
    tables = _ssm_tables(ssm_a_re[l].astype(F32), ssm_a_im[l].astype(F32), ssm_log_dt[l].astype(F32),
                         ssm_b_re[l].astype(F32), ssm_b_im[l].astype(F32),
                         ssm_c_re[l].astype(F32), ssm_c_im[l].astype(F32))
    wr = jnp.concatenate([w_r1[l].astype(F32), w_r2[l].astype(F32).reshape(d, N_EXPERTS)], axis=1)
    wr = jnp.pad(wr, ((0, 0), (0, LANES - wr.shape[1])))
    wr_hi = wr.astype(BF16)
    wr2 = jnp.concatenate([wr_hi, (wr - wr_hi.astype(F32)).astype(BF16)], axis=1)
    br = jnp.concatenate([b_r1[l].astype(F32), b_r2[l].astype(F32).reshape(N_EXPERTS)])
    br = jnp.pad(br, (0, LANES - br.shape[0])).reshape(1, LANES)
    params = (
        w_in[l].astype(BF16), tables, ssm_d[l].astype(F32).reshape(1, -1),
        w_glu[l].astype(BF16), b_glu[l].astype(F32).reshape(1, -1), rel_bias,
        subln_g[l].astype(F32).reshape(1, -1), w_out[l].astype(BF16),
        ln1_g[l].astype(F32).reshape(1, -1), ln1_b[l].astype(F32).reshape(1, -1),
        wr_hi, wr2, br,
        w_gate[l], w_up[l], w_down[l],
        ln2_g[l].astype(F32).reshape(1, -1), ln2_b[l].astype(F32).reshape(1, -1),
    )
    zeros = jnp.zeros((bp, n_grp * n_state), F32)
    yp, kp, vp, hp_re, hp_im = _stream(x_prompt, zeros, zeros, None, None, params, lam, lam_init)
    ys, ks, vs, hs_re, hs_im = _stream(
        x_sample, state_ssm_re[l].astype(F32).reshape(bs, -1), state_ssm_im[l].astype(F32).reshape(bs, -1),
        cache_k[l].reshape(bs, past, nh * dqk), cache_v[l].reshape(bs, past, -1), params, lam, lam_init)
    return (yp, ys,
            kp.reshape(1, bp, lp, nh, dqk).astype(cache_k.dtype),
            vp.reshape(1, bp, lp, nh, -1).astype(cache_v.dtype),
            hp_re.reshape(1, bp, n_grp, n_state).astype(state_ssm_re.dtype),
            hp_im.reshape(1, bp, n_grp, n_state).astype(state_ssm_im.dtype),
            ks.reshape(1, bs, ls, nh, dqk).astype(cache_k.dtype),
            vs.reshape(1, bs, ls, nh, -1).astype(cache_v.dtype),
            hs_re.reshape(1, bs, n_grp, n_state).astype(state_ssm_re.dtype),
            hs_im.reshape(1, bs, n_grp, n_state).astype(state_ssm_im.dtype))
```

```python
import functools
import math

import jax
import jax.numpy as jnp
from jax import lax
from jax.experimental import pallas as pl
from jax.experimental.pallas import tpu as pltpu

F32 = jnp.float32
BF16 = jnp.bfloat16

CHUNK = 64
SSM_GROUP_CH = 16
SSM_STATE = 64
N_ATTN_HEADS = 8
ATTN_HEAD_DIM = 64
REL_BUCKETS = 32
REL_MAX_DIST = 128
N_EXPERT_GROUPS = 4
EXPERTS_PER_GROUP = 8
N_EXPERTS = N_EXPERT_GROUPS * EXPERTS_PER_GROUP
DEPTH = 1
DN_ALPHA = (2 * DEPTH) ** 0.25
LN_EPS = 1e-5
RMS_EPS = 1e-5
NEG_INF = -1e30
LOG2E = math.log2(math.e)

LANES = 128
VMEM_LIMIT = 56 * 1024 * 1024

SSM_SUB = 16
SSM_OCT = LANES // SSM_GROUP_CH
ATTN_TILE = 1024
ATTN_COLS = 512
MOE_ROWS = 512
MOE_ROWS_SMALL = 128
DMA_UNROLL = 8


def _cparams(sem, vmem=VMEM_LIMIT):
    return pltpu.CompilerParams(dimension_semantics=sem, vmem_limit_bytes=vmem)


def _layer_norm(r, g, b):
    mu = jnp.mean(r, axis=-1, keepdims=True)
    rc = r - mu
    var = jnp.mean(rc * rc, axis=-1, keepdims=True)
    return rc * lax.rsqrt(var + LN_EPS) * g + b


def _store_heads_transposed(dst_ref, z):
    for h in range(dst_ref.shape[0]):
        dst_ref[h] = z[:, LANES * h:LANES * (h + 1)].T.astype(BF16)


def _store_heads_split(dst_ref, z):
    for h in range(dst_ref.shape[1]):
        dst_ref[:, h, :] = z[:, LANES * h:LANES * (h + 1)]


def _in_proj_kernel(x_ref, w_ref, u_ref, k_ref, v_ref, qb_ref, kb_ref, vb_ref, qt_ref, vt_ref, xb_scr,
                    *, q_scale):
    j = pl.program_id(1)

    @pl.when(j == 0)
    def _():
        xb_scr[...] = x_ref[...].astype(BF16)

    n = u_ref.shape[1]
    w = w_ref[:, pl.ds(pl.multiple_of(j * n, n), n)]
    z = jnp.dot(xb_scr[...], w, preferred_element_type=F32)

    @pl.when(j == 0)
    def _():
        u_ref[...] = z

    @pl.when(j == 1)
    def _():
        zq = z * q_scale
        qb_ref[...] = zq.astype(BF16)
        _store_heads_transposed(qt_ref, zq * LOG2E)

    @pl.when(j == 2)
    def _():
        _store_heads_split(k_ref, z)
        kb_ref[...] = z.astype(BF16)

    @pl.when(j == 3)
    def _():
        _store_heads_split(v_ref, z)
        vb_ref[...] = z.astype(BF16)
        _store_heads_transposed(vt_ref, z)


def _in_proj(x2d, w_in_b, tm):
    t, d = x2d.shape
    n = w_in_b.shape[1] // 4
    nh = n // LANES
    row = lambda i, j: (i, 0)
    f32_out = jax.ShapeDtypeStruct((t, n), F32)
    b16_out = jax.ShapeDtypeStruct((t, n), BF16)
    tr_out = jax.ShapeDtypeStruct((nh, LANES, t), BF16)
    tr_spec = pl.BlockSpec((nh, LANES, tm), lambda i, j: (0, 0, i))
    cache_out = jax.ShapeDtypeStruct((t, nh, LANES), F32)
    cache_spec = pl.BlockSpec((tm, nh, LANES), lambda i, j: (i, 0, 0))
    return pl.pallas_call(
        functools.partial(_in_proj_kernel, q_scale=ATTN_HEAD_DIM ** -0.5),
        grid=(t // tm, 4),
        in_specs=[pl.BlockSpec((tm, d), row),
                  pl.BlockSpec((d, 4 * n), lambda i, j: (0, 0), pipeline_mode=pl.Buffered(1))],
        out_specs=[pl.BlockSpec((tm, n), row), cache_spec, cache_spec] + [pl.BlockSpec((tm, n), row)] * 3
        + [tr_spec] * 2,
        out_shape=[f32_out, cache_out, cache_out, b16_out, b16_out, b16_out, tr_out, tr_out],
        scratch_shapes=[pltpu.VMEM((tm, d), BF16)],
        compiler_params=_cparams(("arbitrary", "arbitrary")),
    )(x2d, w_in_b)


def _ssm_tables(a_re, a_im, log_dt, b_re, b_im, c_re, c_im):
    hp = lax.Precision.HIGHEST
    g, p = a_re.shape
    c = b_re.shape[-1]
    s, o = SSM_SUB, SSM_OCT
    dt = jnp.exp(log_dt)[:, None]
    mag = jnp.exp(a_re * dt)
    ab_re = mag * jnp.cos(a_im * dt)
    ab_im = mag * jnp.sin(a_im * dt)
    nr = ab_re - 1.0
    ni = ab_im
    den = a_re * a_re + a_im * a_im
    z_re = ((nr * a_re + ni * a_im) / den)[..., None]
    z_im = ((ni * a_re - nr * a_im) / den)[..., None]
    bb_re = z_re * b_re - z_im * b_im
    bb_im = z_re * b_im + z_im * b_re
    n = jnp.arange(s + 1, dtype=F32)[:, None, None]
    pmag = jnp.exp(n * (a_re * dt))
    pw_re = pmag * jnp.cos(n * (a_im * dt))
    pw_im = pmag * jnp.sin(n * (a_im * dt))
    ca_re = c_re[None] * pw_re[:, :, None, :] - c_im[None] * pw_im[:, :, None, :]
    ca_im = c_re[None] * pw_im[:, :, None, :] + c_im[None] * pw_re[:, :, None, :]
    kern = (jnp.einsum('dgop,gpi->dgio', ca_re[:s], bb_re, precision=hp)
            - jnp.einsum('dgop,gpi->dgio', ca_im[:s], bb_im, precision=hp))
    n_oct = g // o

    def expand_groups(tbl, col_group):
        j, n = tbl.shape[1], tbl.shape[3]
        full = jnp.broadcast_to(tbl[:, :, None], (n_oct, j, o, c, n)).reshape(n_oct, j * o * c, n)
        row_l = (lax.broadcasted_iota(jnp.int32, (j * o * c, n), 0) // c) % o
        col_l = col_group(lax.broadcasted_iota(jnp.int32, (j * o * c, n), 1))
        return jnp.where((row_l == col_l)[None], full, 0.0).astype(BF16)

    kt = jnp.transpose(kern.reshape(s, n_oct, o, c, c), (1, 3, 0, 2, 4))
    lags = expand_groups(kt.reshape(n_oct, 1, c, s * o * c), lambda col: (col % LANES) // c)
    m = jnp.concatenate(
        [jnp.pad(lags[:, :, :LANES * (s - j)], ((0, 0), (0, 0), (LANES * j, 0))) for j in range(s)],
        axis=1)
    nrev = (s - 1) - jnp.arange(s, dtype=F32)[:, None, None]
    rmag = jnp.exp(nrev * (a_re * dt))
    rv_re = rmag * jnp.cos(nrev * (a_im * dt))
    rv_im = rmag * jnp.sin(nrev * (a_im * dt))
    wb_re = rv_re[..., None] * bb_re[None] - rv_im[..., None] * bb_im[None]
    wb_im = rv_re[..., None] * bb_im[None] + rv_im[..., None] * bb_re[None]
    state_group = lambda col: (col % (o * p)) // p
    wb = jnp.stack([wb_re, wb_im], axis=0).reshape(2, s, n_oct, o, p, c)
    wt = jnp.transpose(wb, (2, 1, 5, 0, 3, 4)).reshape(n_oct, s, c, 2 * o * p)
    w = expand_groups(wt, state_group)
    zc = jnp.stack([ca_re[1:], -ca_im[1:]], axis=0).reshape(2, s, n_oct, o, c, p)
    zt = jnp.transpose(zc, (2, 1, 4, 0, 3, 5)).reshape(n_oct, s, c, 2 * o * p)
    z_t = expand_groups(zt, state_group)
    return m, w, z_t, a_re * dt, a_im * dt


def _state_powers(exp_re, exp_im, n_pow, n_oct):
    n = (SSM_SUB * jnp.arange(n_pow, dtype=F32))[:, None, None]
    mag = jnp.exp(n * exp_re)
    pw_re = (mag * jnp.cos(n * exp_im)).reshape(n_pow, n_oct, -1)
    pw_im = (mag * jnp.sin(n * exp_im)).reshape(n_pow, n_oct, -1)
    return jnp.swapaxes(pw_re, 0, 1), jnp.swapaxes(pw_im, 0, 1)


def _ssm_kernel(u_ref, m_ref, w_ref, zt_ref, pre_ref, pim_ref, d_ref, h0re_ref, h0im_ref,
                y_ref, hre_ref, him_ref, lhs_scr, v_scr, sin_scr, cre_scr, cim_scr, *, rows, cps):
    i = pl.program_id(1)
    half = cre_scr.shape[1]

    @pl.when(i == 0)
    def _():
        cre_scr[...] = jnp.zeros_like(cre_scr)
        cim_scr[...] = jnp.zeros_like(cim_scr)

    for j in range(SSM_SUB):
        piece = u_ref[pl.ds(j, rows, stride=SSM_SUB), :]
        lhs_scr[:, LANES * j:LANES * (j + 1)] = piece.astype(BF16)
    lhs = lhs_scr[...]
    nt = half // LANES
    v = jnp.dot(lhs, w_ref[0], preferred_element_type=F32)
    for c in range(2 * nt):
        v_scr[c] = v[:, LANES * c:LANES * (c + 1)]
    seg = rows // 8
    a_re = pre_ref[0, 1:2, :]
    a_im = pim_ref[0, 1:2, :]

    def tiles(ref, r, part):
        return jnp.concatenate([ref[part * nt + c, r, :] for c in range(nt)], axis=1)

    def put_tiles(ref, r, part, val):
        for c in range(nt):
            ref[part * nt + c, r, :] = val[:, LANES * c:LANES * (c + 1)]

    def local(t, carry):
        l_re, l_im = carry
        r = pl.ds(t, 8, stride=seg)
        put_tiles(sin_scr, r, 0, l_re)
        put_tiles(sin_scr, r, 1, l_im)
        v_re = tiles(v_scr, r, 0)
        v_im = tiles(v_scr, r, 1)
        return a_re * l_re - a_im * l_im + v_re, a_re * l_im + a_im * l_re + v_im

    zero = jnp.zeros((8, half), F32)
    e_re, e_im = lax.fori_loop(0, seg, local, (zero, zero), unroll=4)

    aseg_re = pre_ref[0, seg:seg + 1, :]
    aseg_im = pim_ref[0, seg:seg + 1, :]
    c_re = cre_scr[...]
    c_im = cim_scr[...]
    ins_re, ins_im = [], []
    for s in range(8):
        first = i * rows + s * seg
        seq = first // cps
        start = (first % cps) == 0
        c_re = jnp.where(start, h0re_ref[pl.ds(seq, 1), :], c_re)
        c_im = jnp.where(start, h0im_ref[pl.ds(seq, 1), :], c_im)
        ins_re.append(c_re)
        ins_im.append(c_im)
        n_re = aseg_re * c_re - aseg_im * c_im + e_re[s:s + 1]
        n_im = aseg_re * c_im + aseg_im * c_re + e_im[s:s + 1]
        hre_ref[pl.ds(seq, 1), :] = n_re
        him_ref[pl.ds(seq, 1), :] = n_im
        c_re, c_im = n_re, n_im
    cre_scr[...] = c_re
    cim_scr[...] = c_im
    in_re = jnp.concatenate(ins_re, axis=0)
    in_im = jnp.concatenate(ins_im, axis=0)

    def carry_in(t, c):
        r = pl.ds(t, 8, stride=seg)
        p_re = pre_ref[0, pl.ds(t, 1), :]
        p_im = pim_ref[0, pl.ds(t, 1), :]
        put_tiles(sin_scr, r, 0, tiles(sin_scr, r, 0) + (p_re * in_re - p_im * in_im))
        put_tiles(sin_scr, r, 1, tiles(sin_scr, r, 1) + (p_re * in_im + p_im * in_re))
        return c

    lax.fori_loop(0, seg, carry_in, 0, unroll=4)

    s_in = jnp.concatenate([sin_scr[c] for c in range(2 * nt)], axis=1).astype(BF16)
    y = (jnp.dot(lhs, m_ref[0], preferred_element_type=F32)
         + lax.dot_general(s_in, zt_ref[0], (((1,), (1,)), ((), ())), preferred_element_type=F32))
    d = d_ref[...]
    for j in range(SSM_SUB):
        uj = u_ref[pl.ds(j, rows, stride=SSM_SUB), :]
        y_ref[pl.ds(j, rows, stride=SSM_SUB), :] = y[:, LANES * j:LANES * (j + 1)] + d * uj


def _ssm(u, tables, d_row, h0_re, h0_im, seq_len, tm):
    m, w, z, exp_re, exp_im = tables
    t, width = u.shape
    n_oct = width // LANES
    nseq = h0_re.shape[0]
    rows = tm // SSM_SUB
    cps = seq_len // SSM_SUB
    seg = rows // 8
    assert rows % 8 == 0 and cps % seg == 0
    pw_re, pw_im = _state_powers(exp_re, exp_im, seg + 1, n_oct)
    half = pw_re.shape[-1]
    kdim = SSM_SUB * LANES
    once = pl.Buffered(1)
    col = lambda g, i: (i, g)
    per_oct3 = lambda g, i: (g, 0, 0)
    per_oct2 = lambda g, i: (0, g)
    return pl.pallas_call(
        functools.partial(_ssm_kernel, rows=rows, cps=cps),
        grid=(n_oct, t // tm),
        in_specs=[
            pl.BlockSpec((tm, LANES), col),
            pl.BlockSpec((1, kdim, kdim), per_oct3, pipeline_mode=once),
            pl.BlockSpec((1, kdim, 2 * half), per_oct3, pipeline_mode=once),
            pl.BlockSpec((1, kdim, 2 * half), per_oct3, pipeline_mode=once),
            pl.BlockSpec((1, seg + 1, half), per_oct3),
            pl.BlockSpec((1, seg + 1, half), per_oct3),
            pl.BlockSpec((1, LANES), per_oct2),
            pl.BlockSpec((nseq, half), per_oct2),
            pl.BlockSpec((nseq, half), per_oct2),
        ],
        out_specs=[
            pl.BlockSpec((tm, LANES), col),
            pl.BlockSpec((nseq, half), per_oct2),
            pl.BlockSpec((nseq, half), per_oct2),
        ],
        out_shape=[
            jax.ShapeDtypeStruct((t, width), F32),
            jax.ShapeDtypeStruct((nseq, n_oct * half), F32),
            jax.ShapeDtypeStruct((nseq, n_oct * half), F32),
        ],
        scratch_shapes=[
            pltpu.VMEM((rows, kdim), BF16),
            pltpu.VMEM((2 * half // LANES, rows, LANES), F32),
            pltpu.VMEM((2 * half // LANES, rows, LANES), F32),
            pltpu.VMEM((1, half), F32),
            pltpu.VMEM((1, half), F32),
        ],
        compiler_params=_cparams(("arbitrary", "arbitrary")),
    )(u, m, w, z, pw_re, pw_im, d_row, h0_re, h0_im)


def _rel_bucket(rel):
    nb = REL_BUCKETS // 2
    max_exact = nb // 2
    n = jnp.abs(rel)
    nf = jnp.maximum(n, 1).astype(F32)
    large = max_exact + (jnp.log(nf / max_exact) / math.log(REL_MAX_DIST / max_exact)
                         * (nb - max_exact)).astype(jnp.int32)
    large = jnp.minimum(large, nb - 1)
    return jnp.where(rel > 0, nb, 0) + jnp.where(n < max_exact, n, large)


def _bias_table(rel_bias, n_k, n_q, q0, shift):
    k_pos = lax.broadcasted_iota(jnp.int32, (n_k, n_q), 0)
    q_pos = q0 + lax.broadcasted_iota(jnp.int32, (n_k, n_q), 1)
    bucket = _rel_bucket(k_pos - q_pos)[None]
    rows = (rel_bias.astype(F32) - shift[None, :])[:, :, None, None]
    bias = jnp.broadcast_to(rows[0], (rel_bias.shape[1], n_k, n_q))
    for b in range(1, REL_BUCKETS):
        bias = jnp.where(bucket == b, rows[b], bias)
    return jnp.where((k_pos // CHUNK <= q_pos // CHUNK)[None], bias, NEG_INF)


def _split_heads_lhs(q):
    lane = lax.broadcasted_iota(jnp.int32, q.shape, 1)
    zero = jnp.zeros_like(q)
    return jnp.concatenate([jnp.where(lane < ATTN_HEAD_DIM, q, zero),
                            jnp.where(lane >= ATTN_HEAD_DIM, q, zero)], axis=0)


def _nt_dot(a, b):
    return lax.dot_general(a, b, (((1,), (1,)), ((), ())), preferred_element_type=F32)


def _diff_finish(acc, l, lam, g, tq, post_scale):
    o = acc[:tq] / l[:tq] - lam * (acc[tq:] / l[tq:])
    ms = jnp.mean(o * o, axis=-1, keepdims=True)
    return o * lax.rsqrt(ms + RMS_EPS) * g * post_scale


def _flash_kernel(lam_ref, qt_ref, k_ref, vt_ref, bias_ref, g_ref, o_ref,
                  lhs_scr, s0_scr, s1_scr, m_scr, acc_scr, *, tq, tk, cb, post_scale):
    qi = pl.program_id(2)
    qt = qt_ref[0]
    row = lax.broadcasted_iota(jnp.int32, qt.shape, 0)
    zero = jnp.zeros_like(qt)
    lhs_scr[:, 0:tq] = jnp.where(row < ATTN_HEAD_DIM, qt, zero)
    lhs_scr[:, tq:2 * tq] = jnp.where(row >= ATTN_HEAD_DIM, qt, zero)
    m_scr[...] = jnp.full_like(m_scr, NEG_INF)
    acc_scr[...] = jnp.zeros_like(acc_scr)
    ones_rows = jnp.ones((acc_scr.shape[0] - LANES, tk), BF16)
    s_bufs = (s0_scr, s1_scr)

    all_blocks = tuple(range(2 * tq // cb))
    late_blocks = tuple(c for c in all_blocks if (c * cb) % tq >= tk)

    def qk(tile, buf, blocks=None):
        k = k_ref[pl.ds(pl.multiple_of(tile * tk, tk), tk), :]
        if blocks is None:
            s_bufs[buf][...] = jnp.dot(k, lhs_scr[...], preferred_element_type=F32)
        else:
            for c in blocks:
                cols = pl.ds(c * cb, cb)
                s_bufs[buf][:, cols] = jnp.dot(k, lhs_scr[:, cols], preferred_element_type=F32)

    def softmax_pv(tile, buf, bias, blocks=all_blocks):
        vt = vt_ref[0, :, pl.ds(pl.multiple_of(tile * tk, tk), tk)]
        vt = jnp.concatenate([vt, ones_rows], axis=0)
        for c in blocks:
            cols = pl.ds(c * cb, cb)
            s = s_bufs[buf][:, cols]
            if bias is not None:
                b0 = (c * cb) % tq
                s = s + bias[:, b0:b0 + cb]
            m_prev = m_scr[:, cols]
            m_new = jnp.maximum(m_prev, jnp.max(s, axis=0, keepdims=True))
            alpha = jnp.exp2(m_prev - m_new)
            p = jnp.exp2(s - m_new).astype(BF16)
            acc_scr[:, cols] = alpha * acc_scr[:, cols] + jnp.dot(vt, p, preferred_element_type=F32)
            m_scr[:, cols] = m_new

    qk(0, 0)

    def far_pair(p, carry):
        qk(2 * p + 1, 1)
        softmax_pv(2 * p, 0, None)
        qk(2 * p + 2, 0)
        softmax_pv(2 * p + 1, 1, None)
        return carry

    lax.fori_loop(0, jnp.maximum(qi - 1, 0), far_pair, 0)

    @pl.when(qi > 0)
    def _():
        qk(2 * qi - 1, 1)
        softmax_pv(2 * qi - 2, 0, None)
        qk(2 * qi, 0)
        softmax_pv(2 * qi - 1, 1, bias_ref[0, 0])

    qk(2 * qi + 1, 1, late_blocks)
    softmax_pv(2 * qi, 0, bias_ref[0, 1])
    softmax_pv(2 * qi + 1, 1, bias_ref[0, 2], late_blocks)
    acc = acc_scr[...]
    l = acc[LANES:LANES + 1]
    o = acc[:LANES, :tq] / l[:, :tq] - lam_ref[0] * (acc[:LANES, tq:] / l[:, tq:])
    ms = jnp.mean(o * o, axis=0, keepdims=True)
    o = o * lax.rsqrt(ms + RMS_EPS) * (g_ref[...] * post_scale)
    o_ref[...] = o.T.astype(o_ref.dtype)


def _attn_prompt(qt, kb, vt, rel_bias, lam, g, batch, seq_len, post_scale, tq):
    tk = tq // 2
    assert tk >= REL_MAX_DIST and tk % CHUNK == 0 and seq_len % tq == 0
    nh, _, t = qt.shape
    nq = seq_len // tq
    far_bias = rel_bias.astype(F32)[_rel_bucket(jnp.int32(-(tk + 1)))]
    table = (_bias_table(rel_bias, 3 * tk, tq, tk, far_bias) * LOG2E).reshape(nh, 3, tk, tq)
    g_col = jnp.broadcast_to(g.reshape(LANES, 1), (LANES, tq))
    acc_rows = LANES + 16
    return pl.pallas_call(
        functools.partial(_flash_kernel, tq=tq, tk=tk, cb=ATTN_COLS, post_scale=post_scale),
        grid=(batch, nh, nq),
        in_specs=[
            pl.BlockSpec(memory_space=pltpu.SMEM),
            pl.BlockSpec((1, LANES, tq), lambda b, h, qi: (h, 0, b * nq + qi)),
            pl.BlockSpec((seq_len, LANES), lambda b, h, qi: (b, h)),
            pl.BlockSpec((1, LANES, seq_len), lambda b, h, qi: (h, 0, b)),
            pl.BlockSpec((1, 3, tk, tq), lambda b, h, qi: (h, 0, 0, 0)),
            pl.BlockSpec((LANES, tq), lambda b, h, qi: (0, 0)),
        ],
        out_specs=pl.BlockSpec((tq, LANES), lambda b, h, qi: (b * nq + qi, h)),
        out_shape=jax.ShapeDtypeStruct((t, nh * LANES), BF16),
        scratch_shapes=[
            pltpu.VMEM((LANES, 2 * tq), BF16),
            pltpu.VMEM((tk, 2 * tq), F32),
            pltpu.VMEM((tk, 2 * tq), F32),
            pltpu.VMEM((1, 2 * tq), F32),
            pltpu.VMEM((acc_rows, 2 * tq), F32),
        ],
        compiler_params=_cparams(("arbitrary", "arbitrary", "arbitrary")),
    )(lam, qt, kb, vt, table, g_col)


def _sample_attn_kernel(lam_ref, q_ref, ck_ref, cv_ref, kn_ref, vn_ref, bias_ref, g_ref, o_ref,
                        *, past, post_scale):
    tq = q_ref.shape[0]
    lhs = _split_heads_lhs(q_ref[...])
    bias = bias_ref[0]
    bias2 = jnp.concatenate([bias, bias], axis=0)
    s_c = _nt_dot(lhs, ck_ref[0].astype(BF16)) + bias2[:, :past]
    s_n = _nt_dot(lhs, kn_ref[...]) + bias2[:, past:]
    m = jnp.maximum(jnp.max(s_c, axis=1, keepdims=True), jnp.max(s_n, axis=1, keepdims=True))
    p_c = jnp.exp(s_c - m)
    p_n = jnp.exp(s_n - m)
    l = jnp.sum(p_c, axis=1, keepdims=True) + jnp.sum(p_n, axis=1, keepdims=True)
    acc = (jnp.dot(p_c.astype(BF16), cv_ref[0].astype(BF16), preferred_element_type=F32)
           + jnp.dot(p_n.astype(BF16), vn_ref[...], preferred_element_type=F32))
    o = _diff_finish(acc, l, lam_ref[0], g_ref[...], tq, post_scale)
    o_ref[...] = o.astype(o_ref.dtype)


def _attn_sample(qb, kb, vb, cache_k, cache_v, rel_bias, lam, g_row, post_scale):
    nstream, past, width = cache_k.shape
    nh = width // LANES
    ls = qb.shape[0] // nstream
    no_shift = jnp.zeros((nh,), F32)
    table = jnp.swapaxes(_bias_table(rel_bias, past + ls, ls, past, no_shift), 1, 2)
    new = lambda b, h: (b, h)
    old = lambda b, h: (b, 0, h)
    return pl.pallas_call(
        functools.partial(_sample_attn_kernel, past=past, post_scale=post_scale),
        grid=(nstream, nh),
        in_specs=[
            pl.BlockSpec(memory_space=pltpu.SMEM),
            pl.BlockSpec((ls, LANES), new),
            pl.BlockSpec((1, past, LANES), old),
            pl.BlockSpec((1, past, LANES), old),
            pl.BlockSpec((ls, LANES), new),
            pl.BlockSpec((ls, LANES), new),
            pl.BlockSpec((1, ls, past + ls), lambda b, h: (h, 0, 0)),
            pl.BlockSpec((1, LANES), lambda b, h: (0, 0)),
        ],
        out_specs=pl.BlockSpec((ls, LANES), new),
        out_shape=jax.ShapeDtypeStruct(qb.shape, BF16),
        compiler_params=_cparams(("arbitrary", "arbitrary")),
    )(lam, qb, cache_k, cache_v, kb, vb, table, g_row)


def _mix_kernel(x_ref, y_ref, a_ref, wglu_ref, bglu_ref, wout_ref, g1_ref, b1_ref,
                wrh_ref, wr2_ref, br_ref, h_ref, hp_ref, lg_ref):
    half = y_ref.shape[1]
    g = jax.nn.gelu(y_ref[...])
    t = jnp.dot(g.astype(BF16), wglu_ref[...], preferred_element_type=F32) + bglu_ref[...]
    so = g * jax.nn.sigmoid(t)
    mix = (jnp.dot(so.astype(BF16), wout_ref[0:half, :], preferred_element_type=F32)
           + jnp.dot(a_ref[...], wout_ref[half:2 * half, :], preferred_element_type=F32))
    h = _layer_norm(DN_ALPHA * x_ref[...] + mix, g1_ref[...], b1_ref[...])
    h_ref[...] = h
    h_hi = h.astype(BF16)
    h_rt = h_hi.astype(F32)
    top = lax.bitcast_convert_type(h_rt[:, :half], jnp.uint32)
    bot = lax.bitcast_convert_type(h_rt[:, half:], jnp.uint32)
    hp_ref[...] = top | (bot >> 16)
    h_lo = (h - h_rt).astype(BF16)
    both = jnp.dot(h_hi, wr2_ref[...], preferred_element_type=F32)
    lg = both[:, :LANES] + both[:, LANES:] + jnp.dot(h_lo, wrh_ref[...], preferred_element_type=F32)
    lg_ref[...] = lg + br_ref[...]


def _mix(x2d, y_ssm, attn, wglu_b, bglu, wout_b, g1, b1, wr_hi, wr2, br, tm):
    t, d = x2d.shape
    half = d // 2
    row = lambda i: (i, 0)
    fix = lambda i: (0, 0)
    once = pl.Buffered(1)
    return pl.pallas_call(
        _mix_kernel,
        grid=(t // tm,),
        in_specs=[
            pl.BlockSpec((tm, d), row),
            pl.BlockSpec((tm, half), row),
            pl.BlockSpec((tm, half), row),
            pl.BlockSpec((half, half), fix, pipeline_mode=once),
            pl.BlockSpec((1, half), fix),
            pl.BlockSpec((d, d), fix, pipeline_mode=once),
            pl.BlockSpec((1, d), fix),
            pl.BlockSpec((1, d), fix),
            pl.BlockSpec((d, LANES), fix),
            pl.BlockSpec((d, 2 * LANES), fix),
            pl.BlockSpec((1, LANES), fix),
        ],
        out_specs=[pl.BlockSpec((tm, d), row), pl.BlockSpec((tm, half), row), pl.BlockSpec((tm, LANES), row)],
        out_shape=[jax.ShapeDtypeStruct((t, d), F32), jax.ShapeDtypeStruct((t, half), jnp.uint32),
                   jax.ShapeDtypeStruct((t, LANES), F32)],
        compiler_params=_cparams(("arbitrary",)),
    )(x2d, y_ssm, attn, wglu_b, bglu, wout_b, g1, b1, wr_hi, wr2, br)


def _route_kernel(lg_ref, dest_ref, gate_ref, cnt_ref, tot_scr, run_scr, start_scr, *, block_rows):
    ph = pl.program_id(0)
    i = pl.program_id(1)
    tm = lg_ref.shape[0]
    ng = N_EXPERT_GROUPS

    @pl.when((ph == 0) & (i == 0))
    def _():
        tot_scr[...] = jnp.zeros_like(tot_scr)

    lg = lg_ref[...]
    lane = lax.broadcasted_iota(jnp.int32, lg.shape, 1)
    is_g = lane < ng
    l1 = jnp.where(is_g, lg, -jnp.inf)
    m1 = jnp.max(l1, axis=1, keepdims=True)
    z1 = jnp.sum(jnp.where(is_g, jnp.exp(l1 - m1), 0.0), axis=1, keepdims=True)
    w_grp = 1.0 / z1
    el = (lane - ng).astype(F32)
    none = float(LANES)
    grp = jnp.min(jnp.where(l1 == m1, lane.astype(F32), none), axis=1, keepdims=True)
    lo = grp * EXPERTS_PER_GROUP
    in_grp = jnp.where(el >= lo, jnp.where(el < lo + EXPERTS_PER_GROUP, 1.0, 0.0), 0.0) > 0.5
    l2 = jnp.where(in_grp, lg, -jnp.inf)
    t1 = jnp.max(l2, axis=1, keepdims=True)
    i1 = jnp.min(jnp.where(l2 == t1, el, none), axis=1, keepdims=True)
    l2b = jnp.where(el == i1, -jnp.inf, l2)
    t2 = jnp.max(l2b, axis=1, keepdims=True)
    i2 = jnp.min(jnp.where(l2b == t2, el, none), axis=1, keepdims=True)
    e2 = jnp.exp(t2 - t1)
    den = 1.0 + e2
    g0 = w_grp * (1.0 / den)
    g1 = w_grp * (e2 / den)
    hit1 = el == i1
    hit2 = el == i2
    oh = jnp.where(hit1, 1.0, jnp.where(hit2, 1.0, 0.0))

    @pl.when(ph == 0)
    def _():
        tot_scr[...] = tot_scr[...] + jnp.sum(oh, axis=0, keepdims=True)

    @pl.when(ph == 1)
    def _():
        @pl.when(i == 0)
        def _():
            cnt = tot_scr[...]
            padded = jnp.floor((cnt + (block_rows - 1)) * (1.0 / block_rows)) * block_rows
            padded = jnp.broadcast_to(padded, (8, LANES))
            lane8 = lax.broadcasted_iota(jnp.int32, padded.shape, 1)
            ends = padded
            for sh in (1, 2, 4, 8, 16, 32):
                ends = ends + jnp.where(lane8 >= sh, pltpu.roll(ends, sh, axis=1), 0.0)
            start_scr[...] = (ends - padded)[0:1]
            run_scr[...] = jnp.zeros_like(run_scr)
            cnt_ref[...] = cnt

        r_id = lax.broadcasted_iota(jnp.int32, (tm, tm), 0)
        c_id = lax.broadcasted_iota(jnp.int32, (tm, tm), 1)
        tri = jnp.where(c_id < r_id, 1.0, 0.0).astype(BF16)
        place = (jnp.dot(tri, oh.astype(BF16), preferred_element_type=F32)
                 + (run_scr[...] + start_scr[...]))
        d1 = jnp.sum(jnp.where(hit1, place, 0.0), axis=1, keepdims=True)
        d2 = jnp.sum(jnp.where(hit2, place, 0.0), axis=1, keepdims=True)
        run_scr[...] = run_scr[...] + jnp.sum(oh, axis=0, keepdims=True)
        dmat = jnp.where(lane == 0, d1, jnp.where(lane == 1, d2, 0.0))
        dest_ref[0] = dmat.T[0:8, :].astype(jnp.int32)
        gate_ref[...] = jnp.where(lane == 0, g0, jnp.where(lane == 1, g1, 0.0))


def _route(logits, tm, block_rows):
    t = logits.shape[0]
    placed = lambda ph, i: (ph * i, 0)
    return pl.pallas_call(
        functools.partial(_route_kernel, block_rows=block_rows),
        grid=(2, t // tm),
        in_specs=[pl.BlockSpec((tm, LANES), lambda ph, i: (i, 0))],
        out_specs=[
            pl.BlockSpec((1, 8, tm), lambda ph, i: (ph * i, 0, 0)),
            pl.BlockSpec((tm, LANES), placed),
            pl.BlockSpec((1, LANES), lambda ph, i: (0, 0)),
        ],
        out_shape=[
            jax.ShapeDtypeStruct((t // tm, 8, tm), jnp.int32),
            jax.ShapeDtypeStruct((t, LANES), F32),
            jax.ShapeDtypeStruct((1, LANES), F32),
        ],
        scratch_shapes=[pltpu.VMEM((1, LANES), F32)] * 3,
        compiler_params=_cparams(("arbitrary", "arbitrary")),
    )(logits)


def _row_copy(src, s_row, dst, d_row, sem):
    return pltpu.make_async_copy(src.at[pl.ds(s_row, 1), :], dst.at[pl.ds(d_row, 1), :], sem)


def _dispatch_kernel(dest_ref, h_ref, xin_ref, xbuf_ref, sem):
    del xin_ref
    tm = h_ref.shape[0]

    def issue(r, c):
        _row_copy(h_ref, r, xbuf_ref, dest_ref[0, 0, r], sem).start()
        _row_copy(h_ref, r, xbuf_ref, dest_ref[0, 1, r], sem).start()
        return c

    lax.fori_loop(0, tm, issue, 0, unroll=DMA_UNROLL)

    def drain(r, c):
        _row_copy(h_ref, 0, xbuf_ref, 0, sem).wait()
        _row_copy(h_ref, 0, xbuf_ref, 0, sem).wait()
        return c

    lax.fori_loop(0, tm, drain, 0, unroll=DMA_UNROLL)


def _dest_spec(tm_route, tm):
    per = tm_route // tm
    return pl.BlockSpec((1, 8, tm), lambda i: (i // per, 0, i % per), memory_space=pltpu.SMEM)


def _dispatch(h, dest, n_rows, tm):
    t, d = h.shape
    xbuf0 = jnp.zeros((n_rows, d), h.dtype)
    return pl.pallas_call(
        _dispatch_kernel,
        grid=(t // tm,),
        in_specs=[
            _dest_spec(dest.shape[2], tm),
            pl.BlockSpec((tm, d), lambda i: (i, 0)),
            pl.BlockSpec(memory_space=pl.ANY),
        ],
        out_specs=pl.BlockSpec(memory_space=pl.ANY),
        out_shape=jax.ShapeDtypeStruct((n_rows, d), h.dtype),
        scratch_shapes=[pltpu.SemaphoreType.DMA(())],
        input_output_aliases={2: 0},
        compiler_params=_cparams(("arbitrary",)),
    )(dest, h, xbuf0)


def _expert_kernel(be_ref, nused_ref, x_ref, wg_ref, wu_ref, wd_ref, y_ref, wg_scr, wu_scr, wd_scr):
    i = pl.program_id(0)
    active = i < nused_ref[0]
    new_expert = (i == 0) | (be_ref[i] != be_ref[jnp.maximum(i - 1, 0)])

    @pl.when(active & new_expert)
    def _():
        wg_scr[...] = wg_ref[0].astype(BF16)
        wu_scr[...] = wu_ref[0].astype(BF16)
        wd_scr[...] = wd_ref[0].astype(BF16)

    @pl.when(active)
    def _():
        xp = x_ref[...]
        top = lax.bitcast_convert_type(xp & jnp.uint32(0xFFFF0000), F32).astype(BF16)
        bot = lax.bitcast_convert_type(xp << 16, F32).astype(BF16)
        x = jnp.concatenate([top, bot], axis=1)
        gt = jnp.dot(x, wg_scr[...], preferred_element_type=F32)
        up = jnp.dot(x, wu_scr[...], preferred_element_type=F32)
        a = (jax.nn.silu(gt) * up).astype(BF16)
        y_ref[...] = jnp.dot(a, wd_scr[...], preferred_element_type=F32)

    @pl.when(i >= nused_ref[0])
    def _():
        y_ref[...] = jnp.zeros_like(y_ref)


def _experts(xbuf, block_e, n_used, wg_b, wu_b, wd_b, block_rows):
    n_rows = xbuf.shape[0]
    d, de = wg_b.shape[1], wg_b.shape[2]
    nb = n_rows // block_rows
    wmap = lambda i, be, nu: (be[i], 0, 0)
    grid_spec = pltpu.PrefetchScalarGridSpec(
        num_scalar_prefetch=2,
        grid=(nb,),
        in_specs=[
            pl.BlockSpec((block_rows, d // 2), lambda i, be, nu: (i, 0)),
            pl.BlockSpec((1, d, de), wmap),
            pl.BlockSpec((1, d, de), wmap),
            pl.BlockSpec((1, de, d), wmap),
        ],
        out_specs=pl.BlockSpec((block_rows, d), lambda i, be, nu: (i, 0)),
        scratch_shapes=[pltpu.VMEM((d, de), BF16), pltpu.VMEM((d, de), BF16), pltpu.VMEM((de, d), BF16)],
    )
    return pl.pallas_call(
        _expert_kernel,
        grid_spec=grid_spec,
        out_shape=jax.ShapeDtypeStruct((n_rows, d), F32),
        compiler_params=_cparams(("arbitrary",)),
    )(block_e, n_used, xbuf, wg_b, wu_b, wd_b)


def _combine_kernel(dest_ref, h_ref, gate_ref, g2_ref, b2_ref, ybuf_ref, o_ref, ya_scr, yb_scr, sem):
    tm = h_ref.shape[0]

    def issue(r, c):
        _row_copy(ybuf_ref, dest_ref[0, 0, r], ya_scr, r, sem).start()
        _row_copy(ybuf_ref, dest_ref[0, 1, r], yb_scr, r, sem).start()
        return c

    lax.fori_loop(0, tm, issue, 0, unroll=DMA_UNROLL)

    def drain(r, c):
        _row_copy(ybuf_ref, 0, ya_scr, 0, sem).wait()
        _row_copy(ybuf_ref, 0, yb_scr, 0, sem).wait()
        return c

    lax.fori_loop(0, tm, drain, 0, unroll=DMA_UNROLL)
    gate = gate_ref[...]
    f = ya_scr[...] * gate[:, 0:1] + yb_scr[...] * gate[:, 1:2]
    o_ref[...] = _layer_norm(DN_ALPHA * h_ref[...] + f, g2_ref[...], b2_ref[...])


def _combine(h, gate, dest, ybuf, g2, b2, tm):
    t, d = h.shape
    row = lambda i: (i, 0)
    fix = lambda i: (0, 0)
    return pl.pallas_call(
        _combine_kernel,
        grid=(t // tm,),
        in_specs=[
            _dest_spec(dest.shape[2], tm),
            pl.BlockSpec((tm, d), row),
            pl.BlockSpec((tm, LANES), row),
            pl.BlockSpec((1, d), fix),
            pl.BlockSpec((1, d), fix),
            pl.BlockSpec(memory_space=pl.ANY),
        ],
        out_specs=pl.BlockSpec((tm, d), row),
        out_shape=jax.ShapeDtypeStruct((t, d), F32),
        scratch_shapes=[pltpu.VMEM((tm, d), F32), pltpu.VMEM((tm, d), F32), pltpu.SemaphoreType.DMA(())],
        compiler_params=_cparams(("arbitrary",)),
    )(dest, h, gate, g2, b2, ybuf)


def _moe_finish(h, hp, logits, wg_b, wu_b, wd_b, g2, b2, tm_route, tm_rows):
    t = h.shape[0]
    rows = MOE_ROWS if 2 * t >= N_EXPERTS * MOE_ROWS else MOE_ROWS_SMALL
    dest, gate, cnt = _route(logits, tm_route, rows)
    counts = cnt[0, N_EXPERT_GROUPS:N_EXPERT_GROUPS + N_EXPERTS].astype(jnp.int32)
    pad_end = jnp.cumsum((counts + rows - 1) // rows * rows)
    n_blocks = -(-(2 * t) // rows) + N_EXPERTS
    first_row = jnp.arange(n_blocks, dtype=jnp.int32)[:, None] * rows
    block_e = jnp.minimum(jnp.sum((pad_end[None, :] <= first_row).astype(jnp.int32), axis=1),
                          N_EXPERTS - 1)
    n_used = (pad_end[-1:] // rows).astype(jnp.int32)
    xbuf = _dispatch(hp, dest, n_blocks * rows, tm_rows)
    ybuf = _experts(xbuf, block_e, n_used, wg_b, wu_b, wd_b, rows)
    return _combine(h, gate, dest, ybuf, g2, b2, tm_rows)


def _pick(t, pref):
    tm = min(pref, t)
    while t % tm:
        tm //= 2
    return tm


def _stream(x, h0_re, h0_im, cache_k, cache_v, params, lam, lam_init):
    (w_in_b, tables, d_row, wglu_b, bglu, rel_bias, g_row, wout_b, g1, b1,
     wr_hi, wr2, br, wg_b, wu_b, wd_b, g2, b2) = params
    b, l, d = x.shape
    t = b * l
    x2d = x.reshape(t, d)
    u, k, v, qb, kb, vb, qt, vt = _in_proj(x2d, w_in_b, _pick(t, 512))
    y_ssm, h_re, h_im = _ssm(u, tables, d_row, h0_re, h0_im, l, _pick(t, 4096))
    post = 1.0 - lam_init
    if cache_k is None:
        attn = _attn_prompt(qt, kb, vt, rel_bias, lam, g_row, b, l, post, _pick(l, ATTN_TILE))
    else:
        attn = _attn_sample(qb, kb, vb, cache_k, cache_v, rel_bias, lam, g_row, post)
    h, hp, logits = _mix(x2d, y_ssm, attn, wglu_b, bglu, wout_b, g1, b1, wr_hi, wr2, br, _pick(t, 512))
    out = _moe_finish(h, hp, logits, wg_b, wu_b, wd_b, g2, b2, _pick(t, 512), _pick(t, 512))
    return out.reshape(b, l, d), k, v, h_re, h_im


def kernel(x_prompt, x_sample, cache_k, cache_v, state_ssm_re, state_ssm_im, w_in, ssm_a_re, ssm_a_im, ssm_log_dt, ssm_b_re, ssm_b_im, ssm_c_re, ssm_c_im, ssm_d, w_glu, b_glu, lambda_q1, lambda_k1, lambda_q2, lambda_k2, subln_g, rel_bias, w_out, ln1_g, ln1_b, w_r1, b_r1, w_r2, b_r2, w_gate, w_up, w_down, ln2_g, ln2_b):
    assert w_in.shape[0] == DEPTH
    bp, lp, d = x_prompt.shape
    bs, ls, _ = x_sample.shape
    past = cache_k.shape[2]
    nh, dqk = cache_k.shape[3], cache_k.shape[4]
    n_grp, n_state = state_ssm_re.shape[2], state_ssm_re.shape[3]
    l = 0
    lam_init = 0.8 - 0.6 * math.exp(-0.3 * l)
    lam = (jnp.exp(jnp.sum(lambda_q1[l].astype(F32) * lambda_k1[l].astype(F32)))
           - jnp.exp(jnp.sum(lambda_q2[l].astype(F32) * lambda_k2[l].astype(F32))) + lam_init).reshape(1)
    tables = _ssm_tables(ssm_a_re[l].astype(F32), ssm_a_im[l].astype(F32), ssm_log_dt[l].astype(F32),
                         ssm_b_re[l].astype(F32), ssm_b_im[l].astype(F32),
                         ssm_c_re[l].astype(F32), ssm_c_im[l].astype(F32))
    wr = jnp.concatenate([w_r1[l].astype(F32), w_r2[l].astype(F32).reshape(d, N_EXPERTS)], axis=1)
    wr = jnp.pad(wr, ((0, 0), (0, LANES - wr.shape[1])))
    wr_hi = wr.astype(BF16)
    wr2 = jnp.concatenate([wr_hi, (wr - wr_hi.astype(F32)).astype(BF16)], axis=1)
    br = jnp.concatenate([b_r1[l].astype(F32), b_r2[l].astype(F32).reshape(N_EXPERTS)])
    br = jnp.pad(br, (0, LANES - br.shape[0])).reshape(1, LANES)
    params = (
        w_in[l].astype(BF16), tables, ssm_d[l].astype(F32).reshape(1, -1),
        w_glu[l].astype(BF16), b_glu[l].astype(F32).reshape(1, -1), rel_bias,
        subln_g[l].astype(F32).reshape(1, -1), w_out[l].astype(BF16),
        ln1_g[l].astype(F32).reshape(1, -1), ln1_b[l].astype(F32).reshape(1, -1),
        wr_hi, wr2, br,
        w_gate[l], w_up[l], w_down[l],
        ln2_g[l].astype(F32).reshape(1, -1), ln2_b[l].astype(F32).reshape(1, -1),
    )
    zeros = jnp.zeros((bp, n_grp * n_state), F32)
    yp, kp, vp, hp_re, hp_im = _stream(x_prompt, zeros, zeros, None, None, params, lam, lam_init)
    ys, ks, vs, hs_re, hs_im = _stream(
        x_sample, state_ssm_re[l].astype(F32).reshape(bs, -1), state_ssm_im[l].astype(F32).reshape(bs, -1),
        cache_k[l].reshape(bs, past, nh * dqk), cache_v[l].reshape(bs, past, -1), params, lam, lam_init)
    return (yp, ys,
            kp.reshape(1, bp, lp, nh, dqk).astype(cache_k.dtype),
            vp.reshape(1, bp, lp, nh, -1).astype(cache_v.dtype),
            hp_re.reshape(1, bp, n_grp, n_state).astype(state_ssm_re.dtype),
            hp_im.reshape(1, bp, n_grp, n_state).astype(state_ssm_im.dtype),
            ks.reshape(1, bs, ls, nh, dqk).astype(cache_k.dtype),
            vs.reshape(1, bs, ls, nh, -1).astype(cache_v.dtype),
            hs_re.reshape(1, bs, n_grp, n_state).astype(state_ssm_re.dtype),
            hs_im.reshape(1, bs, n_grp, n_state).astype(state_ssm_im.dtype))
```

```python
import functools
import math

import jax
import jax.numpy as jnp
from jax import lax
from jax.experimental import pallas as pl
from jax.experimental.pallas import tpu as pltpu

F32 = jnp.float32
BF16 = jnp.bfloat16

CHUNK = 64
SSM_GROUP_CH = 16
SSM_STATE = 64
N_ATTN_HEADS = 8
ATTN_HEAD_DIM = 64
REL_BUCKETS = 32
REL_MAX_DIST = 128
N_EXPERT_GROUPS = 4
EXPERTS_PER_GROUP = 8
N_EXPERTS = N_EXPERT_GROUPS * EXPERTS_PER_GROUP
DEPTH = 1
DN_ALPHA = (2 * DEPTH) ** 0.25
LN_EPS = 1e-5
RMS_EPS = 1e-5
NEG_INF = -1e30
LOG2E = math.log2(math.e)

LANES = 128
VMEM_LIMIT = 56 * 1024 * 1024

SSM_SUB = 16
SSM_OCT = LANES // SSM_GROUP_CH
ATTN_TILE = 1024
ATTN_COLS = 512
MOE_ROWS = 512
MOE_ROWS_SMALL = 128
DMA_UNROLL = 8


def _cparams(sem, vmem=VMEM_LIMIT):
    return pltpu.CompilerParams(dimension_semantics=sem, vmem_limit_bytes=vmem)


def _layer_norm(r, g, b):
    mu = jnp.mean(r, axis=-1, keepdims=True)
    rc = r - mu
    var = jnp.mean(rc * rc, axis=-1, keepdims=True)
    return rc * lax.rsqrt(var + LN_EPS) * g + b


def _store_heads_transposed(dst_ref, z):
    for h in range(dst_ref.shape[0]):
        dst_ref[h] = z[:, LANES * h:LANES * (h + 1)].T.astype(BF16)


def _store_heads_split(dst_ref, z):
    for h in range(dst_ref.shape[1]):
        dst_ref[:, h, :] = z[:, LANES * h:LANES * (h + 1)]


def _in_proj_kernel(x_ref, w_ref, u_ref, k_ref, v_ref, kb_ref, qa_ref, va_ref, xb_scr,
                    *, q_scale, transposed):
    j = pl.program_id(1)

    @pl.when(j == 0)
    def _():
        xb_scr[...] = x_ref[...].astype(BF16)

    n = u_ref.shape[1]
    w = w_ref[:, pl.ds(pl.multiple_of(j * n, n), n)]
    z = jnp.dot(xb_scr[...], w, preferred_element_type=F32)

    @pl.when(j == 0)
    def _():
        u_ref[...] = z

    @pl.when(j == 1)
    def _():
        zq = z * q_scale
        if transposed:
            _store_heads_transposed(qa_ref, zq * LOG2E)
        else:
            qa_ref[...] = zq.astype(BF16)

    @pl.when(j == 2)
    def _():
        _store_heads_split(k_ref, z)
        kb_ref[...] = z.astype(BF16)

    @pl.when(j == 3)
    def _():
        _store_heads_split(v_ref, z)
        if transposed:
            _store_heads_transposed(va_ref, z)
        else:
            va_ref[...] = z.astype(BF16)


def _in_proj(x2d, w_in_b, tm, transposed):
    t, d = x2d.shape
    n = w_in_b.shape[1] // 4
    nh = n // LANES
    row = lambda i, j: (i, 0)
    f32_out = jax.ShapeDtypeStruct((t, n), F32)
    b16_out = jax.ShapeDtypeStruct((t, n), BF16)
    tr_out = jax.ShapeDtypeStruct((nh, LANES, t), BF16)
    tr_spec = pl.BlockSpec((nh, LANES, tm), lambda i, j: (0, 0, i))
    cache_out = jax.ShapeDtypeStruct((t, nh, LANES), F32)
    cache_spec = pl.BlockSpec((tm, nh, LANES), lambda i, j: (i, 0, 0))
    return pl.pallas_call(
        functools.partial(_in_proj_kernel, q_scale=ATTN_HEAD_DIM ** -0.5, transposed=transposed),
        grid=(t // tm, 4),
        in_specs=[pl.BlockSpec((tm, d), row),
                  pl.BlockSpec((d, 4 * n), lambda i, j: (0, 0), pipeline_mode=pl.Buffered(1))],
        out_specs=[pl.BlockSpec((tm, n), row), cache_spec, cache_spec, pl.BlockSpec((tm, n), row)]
        + [tr_spec if transposed else pl.BlockSpec((tm, n), row)] * 2,
        out_shape=[f32_out, cache_out, cache_out, b16_out] + [tr_out if transposed else b16_out] * 2,
        scratch_shapes=[pltpu.VMEM((tm, d), BF16)],
        compiler_params=_cparams(("arbitrary", "arbitrary")),
    )(x2d, w_in_b)


def _ssm_tables(a_re, a_im, log_dt, b_re, b_im, c_re, c_im):
    hp = lax.Precision.HIGHEST
    g, p = a_re.shape
    c = b_re.shape[-1]
    s, o = SSM_SUB, SSM_OCT
    dt = jnp.exp(log_dt)[:, None]
    mag = jnp.exp(a_re * dt)
    ab_re = mag * jnp.cos(a_im * dt)
    ab_im = mag * jnp.sin(a_im * dt)
    nr = ab_re - 1.0
    ni = ab_im
    den = a_re * a_re + a_im * a_im
    z_re = ((nr * a_re + ni * a_im) / den)[..., None]
    z_im = ((ni * a_re - nr * a_im) / den)[..., None]
    bb_re = z_re * b_re - z_im * b_im
    bb_im = z_re * b_im + z_im * b_re
    n = jnp.arange(s + 1, dtype=F32)[:, None, None]
    pmag = jnp.exp(n * (a_re * dt))
    pw_re = pmag * jnp.cos(n * (a_im * dt))
    pw_im = pmag * jnp.sin(n * (a_im * dt))
    ca_re = c_re[None] * pw_re[:, :, None, :] - c_im[None] * pw_im[:, :, None, :]
    ca_im = c_re[None] * pw_im[:, :, None, :] + c_im[None] * pw_re[:, :, None, :]
    kern = (jnp.einsum('dgop,gpi->dgio', ca_re[:s], bb_re, precision=hp)
            - jnp.einsum('dgop,gpi->dgio', ca_im[:s], bb_im, precision=hp))
    n_oct = g // o

    def expand_groups(tbl, col_group):
        j, n = tbl.shape[1], tbl.shape[3]
        full = jnp.broadcast_to(tbl[:, :, None], (n_oct, j, o, c, n)).reshape(n_oct, j * o * c, n)
        row_l = (lax.broadcasted_iota(jnp.int32, (j * o * c, n), 0) // c) % o
        col_l = col_group(lax.broadcasted_iota(jnp.int32, (j * o * c, n), 1))
        return jnp.where((row_l == col_l)[None], full, 0.0).astype(BF16)

    kt = jnp.transpose(kern.reshape(s, n_oct, o, c, c), (1, 3, 0, 2, 4))
    lags = expand_groups(kt.reshape(n_oct, 1, c, s * o * c), lambda col: (col % LANES) // c)
    m = lags
    nrev = (s - 1) - jnp.arange(s, dtype=F32)[:, None, None]
    rmag = jnp.exp(nrev * (a_re * dt))
    rv_re = rmag * jnp.cos(nrev * (a_im * dt))
    rv_im = rmag * jnp.sin(nrev * (a_im * dt))
    wb_re = rv_re[..., None] * bb_re[None] - rv_im[..., None] * bb_im[None]
    wb_im = rv_re[..., None] * bb_im[None] + rv_im[..., None] * bb_re[None]
    state_group = lambda col: (col % (o * p)) // p
    wb = jnp.stack([wb_re, wb_im], axis=0).reshape(2, s, n_oct, o, p, c)
    wt = jnp.transpose(wb, (2, 1, 5, 0, 3, 4)).reshape(n_oct, s, c, 2 * o * p)
    w = expand_groups(wt, state_group)
    zc = jnp.stack([ca_re[1:], -ca_im[1:]], axis=0).reshape(2, s, n_oct, o, c, p)
    zt = jnp.transpose(zc, (2, 1, 4, 0, 3, 5)).reshape(n_oct, s, c, 2 * o * p)
    z_t = expand_groups(zt, state_group)
    return m, w, z_t, a_re * dt, a_im * dt


def _state_powers(exp_re, exp_im, n_pow, n_oct):
    n = (SSM_SUB * jnp.arange(n_pow, dtype=F32))[:, None, None]
    mag = jnp.exp(n * exp_re)
    pw_re = (mag * jnp.cos(n * exp_im)).reshape(n_pow, n_oct, -1)
    pw_im = (mag * jnp.sin(n * exp_im)).reshape(n_pow, n_oct, -1)
    return jnp.swapaxes(pw_re, 0, 1), jnp.swapaxes(pw_im, 0, 1)


def _ssm_kernel(u_ref, lag_ref, w_ref, zt_ref, pre_ref, pim_ref, d_ref, h0re_ref, h0im_ref,
                y_ref, hre_ref, him_ref, m_scr, lhs_scr, v_scr, sin_scr, cre_scr, cim_scr, *, rows, cps):
    i = pl.program_id(1)
    half = cre_scr.shape[1]

    @pl.when(i == 0)
    def _():
        cre_scr[...] = jnp.zeros_like(cre_scr)
        cim_scr[...] = jnp.zeros_like(cim_scr)
        m_scr[...] = jnp.zeros_like(m_scr)
        for j in range(SSM_SUB):
            m_scr[LANES * j:LANES * (j + 1), LANES * j:] = lag_ref[0, :, :LANES * (SSM_SUB - j)]

    for j in range(SSM_SUB):
        piece = u_ref[pl.ds(j, rows, stride=SSM_SUB), :]
        lhs_scr[:, LANES * j:LANES * (j + 1)] = piece.astype(BF16)
    lhs = lhs_scr[...]
    nt = half // LANES
    v = jnp.dot(lhs, w_ref[0], preferred_element_type=F32)
    for c in range(2 * nt):
        v_scr[c] = v[:, LANES * c:LANES * (c + 1)]
    seg = rows // 8
    a_re = pre_ref[0, 1:2, :]
    a_im = pim_ref[0, 1:2, :]

    def tiles(ref, r, part):
        return jnp.concatenate([ref[part * nt + c, r, :] for c in range(nt)], axis=1)

    def put_tiles(ref, r, part, val):
        for c in range(nt):
            ref[part * nt + c, r, :] = val[:, LANES * c:LANES * (c + 1)]

    def local(t, carry):
        l_re, l_im = carry
        r = pl.ds(t, 8, stride=seg)
        put_tiles(sin_scr, r, 0, l_re)
        put_tiles(sin_scr, r, 1, l_im)
        v_re = tiles(v_scr, r, 0)
        v_im = tiles(v_scr, r, 1)
        return a_re * l_re - a_im * l_im + v_re, a_re * l_im + a_im * l_re + v_im

    zero = jnp.zeros((8, half), F32)
    e_re, e_im = lax.fori_loop(0, seg, local, (zero, zero), unroll=4)

    aseg_re = pre_ref[0, seg:seg + 1, :]
    aseg_im = pim_ref[0, seg:seg + 1, :]
    c_re = cre_scr[...]
    c_im = cim_scr[...]
    ins_re, ins_im = [], []
    for s in range(8):
        first = i * rows + s * seg
        seq = first // cps
        start = (first % cps) == 0
        c_re = jnp.where(start, h0re_ref[pl.ds(seq, 1), :], c_re)
        c_im = jnp.where(start, h0im_ref[pl.ds(seq, 1), :], c_im)
        ins_re.append(c_re)
        ins_im.append(c_im)
        n_re = aseg_re * c_re - aseg_im * c_im + e_re[s:s + 1]
        n_im = aseg_re * c_im + aseg_im * c_re + e_im[s:s + 1]
        hre_ref[pl.ds(seq, 1), :] = n_re
        him_ref[pl.ds(seq, 1), :] = n_im
        c_re, c_im = n_re, n_im
    cre_scr[...] = c_re
    cim_scr[...] = c_im
    in_re = jnp.concatenate(ins_re, axis=0)
    in_im = jnp.concatenate(ins_im, axis=0)

    def carry_in(t, c):
        r = pl.ds(t, 8, stride=seg)
        p_re = pre_ref[0, pl.ds(t, 1), :]
        p_im = pim_ref[0, pl.ds(t, 1), :]
        put_tiles(sin_scr, r, 0, tiles(sin_scr, r, 0) + (p_re * in_re - p_im * in_im))
        put_tiles(sin_scr, r, 1, tiles(sin_scr, r, 1) + (p_re * in_im + p_im * in_re))
        return c

    lax.fori_loop(0, seg, carry_in, 0, unroll=4)

    s_in = jnp.concatenate([sin_scr[c] for c in range(2 * nt)], axis=1).astype(BF16)
    y = (jnp.dot(lhs, m_scr[...], preferred_element_type=F32)
         + lax.dot_general(s_in, zt_ref[0], (((1,), (1,)), ((), ())), preferred_element_type=F32))
    d = d_ref[...]
    for j in range(SSM_SUB):
        uj = u_ref[pl.ds(j, rows, stride=SSM_SUB), :]
        y_ref[pl.ds(j, rows, stride=SSM_SUB), :] = y[:, LANES * j:LANES * (j + 1)] + d * uj


def _ssm(u, tables, d_row, h0_re, h0_im, seq_len, tm):
    m, w, z, exp_re, exp_im = tables
    t, width = u.shape
    n_oct = width // LANES
    nseq = h0_re.shape[0]
    rows = tm // SSM_SUB
    cps = seq_len // SSM_SUB
    seg = rows // 8
    assert rows % 8 == 0 and cps % seg == 0
    pw_re, pw_im = _state_powers(exp_re, exp_im, seg + 1, n_oct)
    half = pw_re.shape[-1]
    kdim = SSM_SUB * LANES
    once = pl.Buffered(1)
    col = lambda g, i: (i, g)
    per_oct3 = lambda g, i: (g, 0, 0)
    per_oct2 = lambda g, i: (0, g)
    return pl.pallas_call(
        functools.partial(_ssm_kernel, rows=rows, cps=cps),
        grid=(n_oct, t // tm),
        in_specs=[
            pl.BlockSpec((tm, LANES), col),
            pl.BlockSpec((1, LANES, kdim), per_oct3),
            pl.BlockSpec((1, kdim, 2 * half), per_oct3, pipeline_mode=once),
            pl.BlockSpec((1, kdim, 2 * half), per_oct3, pipeline_mode=once),
            pl.BlockSpec((1, seg + 1, half), per_oct3),
            pl.BlockSpec((1, seg + 1, half), per_oct3),
            pl.BlockSpec((1, LANES), per_oct2),
            pl.BlockSpec((nseq, half), per_oct2),
            pl.BlockSpec((nseq, half), per_oct2),
        ],
        out_specs=[
            pl.BlockSpec((tm, LANES), col),
            pl.BlockSpec((nseq, half), per_oct2),
            pl.BlockSpec((nseq, half), per_oct2),
        ],
        out_shape=[
            jax.ShapeDtypeStruct((t, width), F32),
            jax.ShapeDtypeStruct((nseq, n_oct * half), F32),
            jax.ShapeDtypeStruct((nseq, n_oct * half), F32),
        ],
        scratch_shapes=[
            pltpu.VMEM((kdim, kdim), BF16),
            pltpu.VMEM((rows, kdim), BF16),
            pltpu.VMEM((2 * half // LANES, rows, LANES), F32),
            pltpu.VMEM((2 * half // LANES, rows, LANES), F32),
            pltpu.VMEM((1, half), F32),
            pltpu.VMEM((1, half), F32),
        ],
        compiler_params=_cparams(("arbitrary", "arbitrary")),
    )(u, m, w, z, pw_re, pw_im, d_row, h0_re, h0_im)


def _rel_bucket(rel):
    nb = REL_BUCKETS // 2
    max_exact = nb // 2
    n = jnp.abs(rel)
    nf = jnp.maximum(n, 1).astype(F32)
    large = max_exact + (jnp.log(nf / max_exact) / math.log(REL_MAX_DIST / max_exact)
                         * (nb - max_exact)).astype(jnp.int32)
    large = jnp.minimum(large, nb - 1)
    return jnp.where(rel > 0, nb, 0) + jnp.where(n < max_exact, n, large)


def _bias_table(rel_bias, n_k, n_q, q0, shift):
    k_pos = lax.broadcasted_iota(jnp.int32, (n_k, n_q), 0)
    q_pos = q0 + lax.broadcasted_iota(jnp.int32, (n_k, n_q), 1)
    bucket = _rel_bucket(k_pos - q_pos)[None]
    rows = (rel_bias.astype(F32) - shift[None, :])[:, :, None, None]
    bias = jnp.broadcast_to(rows[0], (rel_bias.shape[1], n_k, n_q))
    for b in range(1, REL_BUCKETS):
        bias = jnp.where(bucket == b, rows[b], bias)
    return jnp.where((k_pos // CHUNK <= q_pos // CHUNK)[None], bias, NEG_INF)


def _split_heads_lhs(q):
    lane = lax.broadcasted_iota(jnp.int32, q.shape, 1)
    zero = jnp.zeros_like(q)
    return jnp.concatenate([jnp.where(lane < ATTN_HEAD_DIM, q, zero),
                            jnp.where(lane >= ATTN_HEAD_DIM, q, zero)], axis=0)


def _nt_dot(a, b):
    return lax.dot_general(a, b, (((1,), (1,)), ((), ())), preferred_element_type=F32)


def _diff_finish(acc, l, lam, g, tq, post_scale):
    o = acc[:tq] / l[:tq] - lam * (acc[tq:] / l[tq:])
    ms = jnp.mean(o * o, axis=-1, keepdims=True)
    return o * lax.rsqrt(ms + RMS_EPS) * g * post_scale


def _flash_kernel(lam_ref, qt_ref, k_ref, vt_ref, bias_ref, g_ref, o_ref,
                  lhs_scr, s0_scr, s1_scr, m_scr, acc_scr, *, tq, tk, cb, post_scale):
    qi = pl.program_id(2)
    qt = qt_ref[0]
    row = lax.broadcasted_iota(jnp.int32, qt.shape, 0)
    zero = jnp.zeros_like(qt)
    lhs_scr[:, 0:tq] = jnp.where(row < ATTN_HEAD_DIM, qt, zero)
    lhs_scr[:, tq:2 * tq] = jnp.where(row >= ATTN_HEAD_DIM, qt, zero)
    m_scr[...] = jnp.full_like(m_scr, NEG_INF)
    acc_scr[...] = jnp.zeros_like(acc_scr)
    ones_rows = jnp.ones((acc_scr.shape[0] - LANES, tk), BF16)
    s_bufs = (s0_scr, s1_scr)

    all_blocks = tuple(range(2 * tq // cb))
    late_blocks = tuple(c for c in all_blocks if (c * cb) % tq >= tk)

    def qk(tile, buf, blocks=None):
        k = k_ref[pl.ds(pl.multiple_of(tile * tk, tk), tk), :]
        if blocks is None:
            s_bufs[buf][...] = jnp.dot(k, lhs_scr[...], preferred_element_type=F32)
        else:
            for c in blocks:
                cols = pl.ds(c * cb, cb)
                s_bufs[buf][:, cols] = jnp.dot(k, lhs_scr[:, cols], preferred_element_type=F32)

    def softmax_pv(tile, buf, bias, blocks=all_blocks):
        vt = vt_ref[0, :, pl.ds(pl.multiple_of(tile * tk, tk), tk)]
        vt = jnp.concatenate([vt, ones_rows], axis=0)
        for c in blocks:
            cols = pl.ds(c * cb, cb)
            s = s_bufs[buf][:, cols]
            if bias is not None:
                b0 = (c * cb) % tq
                s = s + bias[:, b0:b0 + cb]
            m_prev = m_scr[:, cols]
            m_new = jnp.maximum(m_prev, jnp.max(s, axis=0, keepdims=True))
            alpha = jnp.exp2(m_prev - m_new)
            p = jnp.exp2(s - m_new).astype(BF16)
            acc_scr[:, cols] = alpha * acc_scr[:, cols] + jnp.dot(vt, p, preferred_element_type=F32)
            m_scr[:, cols] = m_new

    qk(0, 0)

    def far_pair(p, carry):
        qk(2 * p + 1, 1)
        softmax_pv(2 * p, 0, None)
        qk(2 * p + 2, 0)
        softmax_pv(2 * p + 1, 1, None)
        return carry

    lax.fori_loop(0, jnp.maximum(qi - 1, 0), far_pair, 0)

    @pl.when(qi > 0)
    def _():
        qk(2 * qi - 1, 1)
        softmax_pv(2 * qi - 2, 0, None)
        qk(2 * qi, 0)
        softmax_pv(2 * qi - 1, 1, bias_ref[0, 0])

    qk(2 * qi + 1, 1, late_blocks)
    softmax_pv(2 * qi, 0, bias_ref[0, 1])
    softmax_pv(2 * qi + 1, 1, bias_ref[0, 2], late_blocks)
    acc = acc_scr[...]
    l = acc[LANES:LANES + 1]
    o = acc[:LANES, :tq] / l[:, :tq] - lam_ref[0] * (acc[:LANES, tq:] / l[:, tq:])
    ms = jnp.mean(o * o, axis=0, keepdims=True)
    o = o * lax.rsqrt(ms + RMS_EPS) * (g_ref[...] * post_scale)
    o_ref[...] = o.T.astype(o_ref.dtype)


def _attn_prompt(qt, kb, vt, rel_bias, lam, g, batch, seq_len, post_scale, tq):
    tk = tq // 2
    assert tk >= REL_MAX_DIST and tk % CHUNK == 0 and seq_len % tq == 0
    nh, _, t = qt.shape
    nq = seq_len // tq
    far_bias = rel_bias.astype(F32)[_rel_bucket(jnp.int32(-(tk + 1)))]
    table = (_bias_table(rel_bias, 3 * tk, tq, tk, far_bias) * LOG2E).reshape(nh, 3, tk, tq)
    g_col = jnp.broadcast_to(g.reshape(LANES, 1), (LANES, tq))
    acc_rows = LANES + 16
    return pl.pallas_call(
        functools.partial(_flash_kernel, tq=tq, tk=tk, cb=ATTN_COLS, post_scale=post_scale),
        grid=(batch, nh, nq),
        in_specs=[
            pl.BlockSpec(memory_space=pltpu.SMEM),
            pl.BlockSpec((1, LANES, tq), lambda b, h, qi: (h, 0, b * nq + qi)),
            pl.BlockSpec((seq_len, LANES), lambda b, h, qi: (b, h)),
            pl.BlockSpec((1, LANES, seq_len), lambda b, h, qi: (h, 0, b)),
            pl.BlockSpec((1, 3, tk, tq), lambda b, h, qi: (h, 0, 0, 0)),
            pl.BlockSpec((LANES, tq), lambda b, h, qi: (0, 0)),
        ],
        out_specs=pl.BlockSpec((tq, LANES), lambda b, h, qi: (b * nq + qi, h)),
        out_shape=jax.ShapeDtypeStruct((t, nh * LANES), BF16),
        scratch_shapes=[
            pltpu.VMEM((LANES, 2 * tq), BF16),
            pltpu.VMEM((tk, 2 * tq), F32),
            pltpu.VMEM((tk, 2 * tq), F32),
            pltpu.VMEM((1, 2 * tq), F32),
            pltpu.VMEM((acc_rows, 2 * tq), F32),
        ],
        compiler_params=_cparams(("arbitrary", "arbitrary", "arbitrary")),
    )(lam, qt, kb, vt, table, g_col)


def _sample_attn_kernel(lam_ref, q_ref, ck_ref, cv_ref, kn_ref, vn_ref, bias_ref, g_ref, o_ref,
                        *, past, post_scale):
    tq = q_ref.shape[0]
    lhs = _split_heads_lhs(q_ref[...])
    bias = bias_ref[0]
    bias2 = jnp.concatenate([bias, bias], axis=0)
    s_c = _nt_dot(lhs, ck_ref[0].astype(BF16)) + bias2[:, :past]
    s_n = _nt_dot(lhs, kn_ref[...]) + bias2[:, past:]
    m = jnp.maximum(jnp.max(s_c, axis=1, keepdims=True), jnp.max(s_n, axis=1, keepdims=True))
    p_c = jnp.exp(s_c - m)
    p_n = jnp.exp(s_n - m)
    l = jnp.sum(p_c, axis=1, keepdims=True) + jnp.sum(p_n, axis=1, keepdims=True)
    acc = (jnp.dot(p_c.astype(BF16), cv_ref[0].astype(BF16), preferred_element_type=F32)
           + jnp.dot(p_n.astype(BF16), vn_ref[...], preferred_element_type=F32))
    o = _diff_finish(acc, l, lam_ref[0], g_ref[...], tq, post_scale)
    o_ref[...] = o.astype(o_ref.dtype)


def _attn_sample(qb, kb, vb, cache_k, cache_v, rel_bias, lam, g_row, post_scale):
    nstream, past, width = cache_k.shape
    nh = width // LANES
    ls = qb.shape[0] // nstream
    no_shift = jnp.zeros((nh,), F32)
    table = jnp.swapaxes(_bias_table(rel_bias, past + ls, ls, past, no_shift), 1, 2)
    new = lambda b, h: (b, h)
    old = lambda b, h: (b, 0, h)
    return pl.pallas_call(
        functools.partial(_sample_attn_kernel, past=past, post_scale=post_scale),
        grid=(nstream, nh),
        in_specs=[
            pl.BlockSpec(memory_space=pltpu.SMEM),
            pl.BlockSpec((ls, LANES), new),
            pl.BlockSpec((1, past, LANES), old),
            pl.BlockSpec((1, past, LANES), old),
            pl.BlockSpec((ls, LANES), new),
            pl.BlockSpec((ls, LANES), new),
            pl.BlockSpec((1, ls, past + ls), lambda b, h: (h, 0, 0)),
            pl.BlockSpec((1, LANES), lambda b, h: (0, 0)),
        ],
        out_specs=pl.BlockSpec((ls, LANES), new),
        out_shape=jax.ShapeDtypeStruct(qb.shape, BF16),
        compiler_params=_cparams(("arbitrary", "arbitrary")),
    )(lam, qb, cache_k, cache_v, kb, vb, table, g_row)


def _mix_kernel(x_ref, y_ref, a_ref, wglu_ref, bglu_ref, wout_ref, g1_ref, b1_ref,
                wrh_ref, wr2_ref, br_ref, h_ref, hp_ref, lg_ref):
    half = y_ref.shape[1]
    g = jax.nn.gelu(y_ref[...])
    t = jnp.dot(g.astype(BF16), wglu_ref[...], preferred_element_type=F32) + bglu_ref[...]
    so = g * jax.nn.sigmoid(t)
    mix = (jnp.dot(so.astype(BF16), wout_ref[0:half, :], preferred_element_type=F32)
           + jnp.dot(a_ref[...], wout_ref[half:2 * half, :], preferred_element_type=F32))
    h = _layer_norm(DN_ALPHA * x_ref[...] + mix, g1_ref[...], b1_ref[...])
    h_ref[...] = h
    h_hi = h.astype(BF16)
    h_rt = h_hi.astype(F32)
    top = lax.bitcast_convert_type(h_rt[:, :half], jnp.uint32)
    bot = lax.bitcast_convert_type(h_rt[:, half:], jnp.uint32)
    hp_ref[...] = top | (bot >> 16)
    h_lo = (h - h_rt).astype(BF16)
    both = jnp.dot(h_hi, wr2_ref[...], preferred_element_type=F32)
    lg = both[:, :LANES] + both[:, LANES:] + jnp.dot(h_lo, wrh_ref[...], preferred_element_type=F32)
    lg_ref[...] = lg + br_ref[...]


def _mix(x2d, y_ssm, attn, wglu_b, bglu, wout_b, g1, b1, wr_hi, wr2, br, tm):
    t, d = x2d.shape
    half = d // 2
    row = lambda i: (i, 0)
    fix = lambda i: (0, 0)
    once = pl.Buffered(1)
    return pl.pallas_call(
        _mix_kernel,
        grid=(t // tm,),
        in_specs=[
            pl.BlockSpec((tm, d), row),
            pl.BlockSpec((tm, half), row),
            pl.BlockSpec((tm, half), row),
            pl.BlockSpec((half, half), fix, pipeline_mode=once),
            pl.BlockSpec((1, half), fix),
            pl.BlockSpec((d, d), fix, pipeline_mode=once),
            pl.BlockSpec((1, d), fix),
            pl.BlockSpec((1, d), fix),
            pl.BlockSpec((d, LANES), fix),
            pl.BlockSpec((d, 2 * LANES), fix),
            pl.BlockSpec((1, LANES), fix),
        ],
        out_specs=[pl.BlockSpec((tm, d), row), pl.BlockSpec((tm, half), row), pl.BlockSpec((tm, LANES), row)],
        out_shape=[jax.ShapeDtypeStruct((t, d), F32), jax.ShapeDtypeStruct((t, half), jnp.uint32),
                   jax.ShapeDtypeStruct((t, LANES), F32)],
        compiler_params=_cparams(("arbitrary",)),
    )(x2d, y_ssm, attn, wglu_b, bglu, wout_b, g1, b1, wr_hi, wr2, br)


def _route_kernel(lg_ref, dest_ref, gate_ref, cnt_ref, tot_scr, run_scr, start_scr, *, block_rows):
    ph = pl.program_id(0)
    i = pl.program_id(1)
    tm = lg_ref.shape[0]
    ng = N_EXPERT_GROUPS

    @pl.when((ph == 0) & (i == 0))
    def _():
        tot_scr[...] = jnp.zeros_like(tot_scr)

    lg = lg_ref[...]
    lane = lax.broadcasted_iota(jnp.int32, lg.shape, 1)
    is_g = lane < ng
    l1 = jnp.where(is_g, lg, -jnp.inf)
    m1 = jnp.max(l1, axis=1, keepdims=True)
    z1 = jnp.sum(jnp.where(is_g, jnp.exp(l1 - m1), 0.0), axis=1, keepdims=True)
    w_grp = 1.0 / z1
    el = (lane - ng).astype(F32)
    none = float(LANES)
    grp = jnp.min(jnp.where(l1 == m1, lane.astype(F32), none), axis=1, keepdims=True)
    lo = grp * EXPERTS_PER_GROUP
    in_grp = jnp.where(el >= lo, jnp.where(el < lo + EXPERTS_PER_GROUP, 1.0, 0.0), 0.0) > 0.5
    l2 = jnp.where(in_grp, lg, -jnp.inf)
    t1 = jnp.max(l2, axis=1, keepdims=True)
    i1 = jnp.min(jnp.where(l2 == t1, el, none), axis=1, keepdims=True)
    l2b = jnp.where(el == i1, -jnp.inf, l2)
    t2 = jnp.max(l2b, axis=1, keepdims=True)
    i2 = jnp.min(jnp.where(l2b == t2, el, none), axis=1, keepdims=True)
    e2 = jnp.exp(t2 - t1)
    den = 1.0 + e2
    g0 = w_grp * (1.0 / den)
    g1 = w_grp * (e2 / den)
    hit1 = el == i1
    hit2 = el == i2
    oh = jnp.where(hit1, 1.0, jnp.where(hit2, 1.0, 0.0))

    @pl.when(ph == 0)
    def _():
        tot_scr[...] = tot_scr[...] + jnp.sum(oh, axis=0, keepdims=True)

    @pl.when(ph == 1)
    def _():
        @pl.when(i == 0)
        def _():
            cnt = tot_scr[...]
            padded = jnp.floor((cnt + (block_rows - 1)) * (1.0 / block_rows)) * block_rows
            padded = jnp.broadcast_to(padded, (8, LANES))
            lane8 = lax.broadcasted_iota(jnp.int32, padded.shape, 1)
            ends = padded
            for sh in (1, 2, 4, 8, 16, 32):
                ends = ends + jnp.where(lane8 >= sh, pltpu.roll(ends, sh, axis=1), 0.0)
            start_scr[...] = (ends - padded)[0:1]
            run_scr[...] = jnp.zeros_like(run_scr)
            cnt_ref[...] = cnt

        r_id = lax.broadcasted_iota(jnp.int32, (tm, tm), 0)
        c_id = lax.broadcasted_iota(jnp.int32, (tm, tm), 1)
        tri = jnp.where(c_id < r_id, 1.0, 0.0).astype(BF16)
        place = (jnp.dot(tri, oh.astype(BF16), preferred_element_type=F32)
                 + (run_scr[...] + start_scr[...]))
        d1 = jnp.sum(jnp.where(hit1, place, 0.0), axis=1, keepdims=True)
        d2 = jnp.sum(jnp.where(hit2, place, 0.0), axis=1, keepdims=True)
        run_scr[...] = run_scr[...] + jnp.sum(oh, axis=0, keepdims=True)
        dmat = jnp.where(lane == 0, d1, jnp.where(lane == 1, d2, 0.0))
        dest_ref[0] = dmat.T[0:8, :].astype(jnp.int32)
        gate_ref[...] = jnp.where(lane == 0, g0, jnp.where(lane == 1, g1, 0.0))


def _route(logits, tm, block_rows):
    t = logits.shape[0]
    placed = lambda ph, i: (ph * i, 0)
    return pl.pallas_call(
        functools.partial(_route_kernel, block_rows=block_rows),
        grid=(2, t // tm),
        in_specs=[pl.BlockSpec((tm, LANES), lambda ph, i: (i, 0))],
        out_specs=[
            pl.BlockSpec((1, 8, tm), lambda ph, i: (ph * i, 0, 0)),
            pl.BlockSpec((tm, LANES), placed),
            pl.BlockSpec((1, LANES), lambda ph, i: (0, 0)),
        ],
        out_shape=[
            jax.ShapeDtypeStruct((t // tm, 8, tm), jnp.int32),
            jax.ShapeDtypeStruct((t, LANES), F32),
            jax.ShapeDtypeStruct((1, LANES), F32),
        ],
        scratch_shapes=[pltpu.VMEM((1, LANES), F32)] * 3,
        compiler_params=_cparams(("arbitrary", "arbitrary")),
    )(logits)


def _row_copy(src, s_row, dst, d_row, sem):
    return pltpu.make_async_copy(src.at[pl.ds(s_row, 1), :], dst.at[pl.ds(d_row, 1), :], sem)


def _dispatch_kernel(dest_ref, h_ref, xin_ref, xbuf_ref, sem):
    del xin_ref
    tm = h_ref.shape[0]

    def issue(r, c):
        _row_copy(h_ref, r, xbuf_ref, dest_ref[0, 0, r], sem).start()
        _row_copy(h_ref, r, xbuf_ref, dest_ref[0, 1, r], sem).start()
        return c

    lax.fori_loop(0, tm, issue, 0, unroll=DMA_UNROLL)

    def drain(r, c):
        _row_copy(h_ref, 0, xbuf_ref, 0, sem).wait()
        _row_copy(h_ref, 0, xbuf_ref, 0, sem).wait()
        return c

    lax.fori_loop(0, tm, drain, 0, unroll=DMA_UNROLL)


def _dest_spec(tm_route, tm):
    per = tm_route // tm
    return pl.BlockSpec((1, 8, tm), lambda i: (i // per, 0, i % per), memory_space=pltpu.SMEM)


def _dispatch(h, dest, n_rows, tm):
    t, d = h.shape
    xbuf0 = jnp.zeros((n_rows, d), h.dtype)
    return pl.pallas_call(
        _dispatch_kernel,
        grid=(t // tm,),
        in_specs=[
            _dest_spec(dest.shape[2], tm),
            pl.BlockSpec((tm, d), lambda i: (i, 0)),
            pl.BlockSpec(memory_space=pl.ANY),
        ],
        out_specs=pl.BlockSpec(memory_space=pl.ANY),
        out_shape=jax.ShapeDtypeStruct((n_rows, d), h.dtype),
        scratch_shapes=[pltpu.SemaphoreType.DMA(())],
        input_output_aliases={2: 0},
        compiler_params=_cparams(("arbitrary",)),
    )(dest, h, xbuf0)


def _expert_kernel(be_ref, nused_ref, x_ref, wg_ref, wu_ref, wd_ref, y_ref, wg_scr, wu_scr, wd_scr):
    i = pl.program_id(0)
    active = i < nused_ref[0]
    new_expert = (i == 0) | (be_ref[i] != be_ref[jnp.maximum(i - 1, 0)])

    @pl.when(active & new_expert)
    def _():
        wg_scr[...] = wg_ref[0].astype(BF16)
        wu_scr[...] = wu_ref[0].astype(BF16)
        wd_scr[...] = wd_ref[0].astype(BF16)

    @pl.when(active)
    def _():
        xp = x_ref[...]
        top = lax.bitcast_convert_type(xp & jnp.uint32(0xFFFF0000), F32).astype(BF16)
        bot = lax.bitcast_convert_type(xp << 16, F32).astype(BF16)
        x = jnp.concatenate([top, bot], axis=1)
        gt = jnp.dot(x, wg_scr[...], preferred_element_type=F32)
        up = jnp.dot(x, wu_scr[...], preferred_element_type=F32)
        a = (jax.nn.silu(gt) * up).astype(BF16)
        y_ref[...] = jnp.dot(a, wd_scr[...], preferred_element_type=F32)

    @pl.when(i >= nused_ref[0])
    def _():
        y_ref[...] = jnp.zeros_like(y_ref)


def _experts(xbuf, block_e, n_used, wg_b, wu_b, wd_b, block_rows):
    n_rows = xbuf.shape[0]
    d, de = wg_b.shape[1], wg_b.shape[2]
    nb = n_rows // block_rows
    wmap = lambda i, be, nu: (be[i], 0, 0)
    grid_spec = pltpu.PrefetchScalarGridSpec(
        num_scalar_prefetch=2,
        grid=(nb,),
        in_specs=[
            pl.BlockSpec((block_rows, d // 2), lambda i, be, nu: (i, 0)),
            pl.BlockSpec((1, d, de), wmap),
            pl.BlockSpec((1, d, de), wmap),
            pl.BlockSpec((1, de, d), wmap),
        ],
        out_specs=pl.BlockSpec((block_rows, d), lambda i, be, nu: (i, 0)),
        scratch_shapes=[pltpu.VMEM((d, de), BF16), pltpu.VMEM((d, de), BF16), pltpu.VMEM((de, d), BF16)],
    )
    return pl.pallas_call(
        _expert_kernel,
        grid_spec=grid_spec,
        out_shape=jax.ShapeDtypeStruct((n_rows, d), F32),
        compiler_params=_cparams(("arbitrary",)),
    )(block_e, n_used, xbuf, wg_b, wu_b, wd_b)


def _combine_kernel(dest_ref, h_ref, gate_ref, g2_ref, b2_ref, ybuf_ref, o_ref, ya_scr, yb_scr, sem):
    tm = h_ref.shape[0]

    def issue(r, c):
        _row_copy(ybuf_ref, dest_ref[0, 0, r], ya_scr, r, sem).start()
        _row_copy(ybuf_ref, dest_ref[0, 1, r], yb_scr, r, sem).start()
        return c

    lax.fori_loop(0, tm, issue, 0, unroll=DMA_UNROLL)

    def drain(r, c):
        _row_copy(ybuf_ref, 0, ya_scr, 0, sem).wait()
        _row_copy(ybuf_ref, 0, yb_scr, 0, sem).wait()
        return c

    lax.fori_loop(0, tm, drain, 0, unroll=DMA_UNROLL)
    gate = gate_ref[...]
    f = ya_scr[...] * gate[:, 0:1] + yb_scr[...] * gate[:, 1:2]
    o_ref[...] = _layer_norm(DN_ALPHA * h_ref[...] + f, g2_ref[...], b2_ref[...])


def _combine(h, gate, dest, ybuf, g2, b2, tm):
    t, d = h.shape
    row = lambda i: (i, 0)
    fix = lambda i: (0, 0)
    return pl.pallas_call(
        _combine_kernel,
        grid=(t // tm,),
        in_specs=[
            _dest_spec(dest.shape[2], tm),
            pl.BlockSpec((tm, d), row),
            pl.BlockSpec((tm, LANES), row),
            pl.BlockSpec((1, d), fix),
            pl.BlockSpec((1, d), fix),
            pl.BlockSpec(memory_space=pl.ANY),
        ],
        out_specs=pl.BlockSpec((tm, d), row),
        out_shape=jax.ShapeDtypeStruct((t, d), F32),
        scratch_shapes=[pltpu.VMEM((tm, d), F32), pltpu.VMEM((tm, d), F32), pltpu.SemaphoreType.DMA(())],
        compiler_params=_cparams(("arbitrary",)),
    )(dest, h, gate, g2, b2, ybuf)


def _moe_finish(h, hp, logits, wg_b, wu_b, wd_b, g2, b2, tm_route, tm_rows):
    t = h.shape[0]
    rows = MOE_ROWS if 2 * t >= N_EXPERTS * MOE_ROWS else MOE_ROWS_SMALL
    dest, gate, cnt = _route(logits, tm_route, rows)
    counts = cnt[0, N_EXPERT_GROUPS:N_EXPERT_GROUPS + N_EXPERTS].astype(jnp.int32)
    pad_end = jnp.cumsum((counts + rows - 1) // rows * rows)
    n_blocks = -(-(2 * t) // rows) + N_EXPERTS
    first_row = jnp.arange(n_blocks, dtype=jnp.int32)[:, None] * rows
    block_e = jnp.minimum(jnp.sum((pad_end[None, :] <= first_row).astype(jnp.int32), axis=1),
                          N_EXPERTS - 1)
    n_used = (pad_end[-1:] // rows).astype(jnp.int32)
    xbuf = _dispatch(hp, dest, n_blocks * rows, tm_rows)
    ybuf = _experts(xbuf, block_e, n_used, wg_b, wu_b, wd_b, rows)
    return _combine(h, gate, dest, ybuf, g2, b2, tm_rows)


def _pick(t, pref):
    tm = min(pref, t)
    while t % tm:
        tm //= 2
    return tm


def _stream(x, h0_re, h0_im, cache_k, cache_v, params, lam, lam_init):
    (w_in_b, tables, d_row, wglu_b, bglu, rel_bias, g_row, wout_b, g1, b1,
     wr_hi, wr2, br, wg_b, wu_b, wd_b, g2, b2) = params
    b, l, d = x.shape
    t = b * l
    x2d = x.reshape(t, d)
    prompt = cache_k is None
    u, k, v, kb, qa, va = _in_proj(x2d, w_in_b, _pick(t, 512), transposed=prompt)
    y_ssm, h_re, h_im = _ssm(u, tables, d_row, h0_re, h0_im, l, _pick(t, 4096))
    post = 1.0 - lam_init
    if prompt:
        attn = _attn_prompt(qa, kb, va, rel_bias, lam, g_row, b, l, post, _pick(l, ATTN_TILE))
    else:
        attn = _attn_sample(qa, kb, va, cache_k, cache_v, rel_bias, lam, g_row, post)
    h, hp, logits = _mix(x2d, y_ssm, attn, wglu_b, bglu, wout_b, g1, b1, wr_hi, wr2, br, _pick(t, 512))
    out = _moe_finish(h, hp, logits, wg_b, wu_b, wd_b, g2, b2, _pick(t, 512), _pick(t, 512))
    return out.reshape(b, l, d), k, v, h_re, h_im


def kernel(x_prompt, x_sample, cache_k, cache_v, state_ssm_re, state_ssm_im, w_in, ssm_a_re, ssm_a_im, ssm_log_dt, ssm_b_re, ssm_b_im, ssm_c_re, ssm_c_im, ssm_d, w_glu, b_glu, lambda_q1, lambda_k1, lambda_q2, lambda_k2, subln_g, rel_bias, w_out, ln1_g, ln1_b, w_r1, b_r1, w_r2, b_r2, w_gate, w_up, w_down, ln2_g, ln2_b):
    assert w_in.shape[0] == DEPTH
    bp, lp, d = x_prompt.shape
    bs, ls, _ = x_sample.shape
    past = cache_k.shape[2]
    nh, dqk = cache_k.shape[3], cache_k.shape[4]
    n_grp, n_state = state_ssm_re.shape[2], state_ssm_re.shape[3]
    l = 0
    lam_init = 0.8 - 0.6 * math.exp(-0.3 * l)
    lam = (jnp.exp(jnp.sum(lambda_q1[l].astype(F32) * lambda_k1[l].astype(F32)))
           - jnp.exp(jnp.sum(lambda_q2[l].astype(F32) * lambda_k2[l].astype(F32))) + lam_init).reshape(1)
    tables = _ssm_tables(ssm_a_re[l].astype(F32), ssm_a_im[l].astype(F32), ssm_log_dt[l].astype(F32),
                         ssm_b_re[l].astype(F32), ssm_b_im[l].astype(F32),
                         ssm_c_re[l].astype(F32), ssm_c_im[l].astype(F32))
    wr = jnp.concatenate([w_r1[l].astype(F32), w_r2[l].astype(F32).reshape(d, N_EXPERTS)], axis=1)
    wr = jnp.pad(wr, ((0, 0), (0, LANES - wr.shape[1])))
    wr_hi = wr.astype(BF16)
    wr2 = jnp.concatenate([wr_hi, (wr - wr_hi.astype(F32)).astype(BF16)], axis=1)
    br = jnp.concatenate([b_r1[l].astype(F32), b_r2[l].astype(F32).reshape(N_EXPERTS)])
    br = jnp.pad(br, (0, LANES - br.shape[0])).reshape(1, LANES)
    params = (
        w_in[l].astype(BF16), tables, ssm_d[l].astype(F32).reshape(1, -1),
        w_glu[l].astype(BF16), b_glu[l].astype(F32).reshape(1, -1), rel_bias,
        subln_g[l].astype(F32).reshape(1, -1), w_out[l].astype(BF16),
        ln1_g[l].astype(F32).reshape(1, -1), ln1_b[l].astype(F32).reshape(1, -1),
        wr_hi, wr2, br,
        w_gate[l], w_up[l], w_down[l],
        ln2_g[l].astype(F32).reshape(1, -1), ln2_b[l].astype(F32).reshape(1, -1),
    )
    zeros = jnp.zeros((bp, n_grp * n_state), F32)
    yp, kp, vp, hp_re, hp_im = _stream(x_prompt, zeros, zeros, None, None, params, lam, lam_init)
    ys, ks, vs, hs_re, hs_im = _stream(
        x_sample, state_ssm_re[l].astype(F32).reshape(bs, -1), state_ssm_im[l].astype(F32).reshape(bs, -1),
        cache_k[l].reshape(bs, past, nh * dqk), cache_v[l].reshape(bs, past, -1), params, lam, lam_init)
    return (yp, ys,
            kp.reshape(1, bp, lp, nh, dqk).astype(cache_k.dtype),
            vp.reshape(1, bp, lp, nh, -1).astype(cache_v.dtype),
            hp_re.reshape(1, bp, n_grp, n_state).astype(state_ssm_re.dtype),
            hp_im.reshape(1, bp, n_grp, n_state).astype(state_ssm_im.dtype),
            ks.reshape(1, bs, ls, nh, dqk).astype(cache_k.dtype),
            vs.reshape(1, bs, ls, nh, -1).astype(cache_v.dtype),
            hs_re.reshape(1, bs, n_grp, n_state).astype(state_ssm_re.dtype),
            hs_im.reshape(1, bs, n_grp, n_state).astype(state_ssm_im.dtype))
```
